```python
import math, functools
import jax, jax.numpy as jnp
from jax import lax
import numpy as np

D_MODEL = 2048
BATCH = 1
SEQ = 8192
DEPTH = 1
DEC_BATCH = 8
DEC_SEQ = 32
PAST_LEN = 2048

CHUNK = 64
HEAD_DIM = 128
N_HEADS_SB = 8
N_HEADS_FOX = 8
D_SB = N_HEADS_SB * HEAD_DIM
D_FOX = N_HEADS_FOX * HEAD_DIM
D_MIX = D_SB + D_FOX
N_IN = 3 * D_SB + 3 * D_FOX + N_HEADS_FOX
D_FF = 5632
PLE_DIM = 256
Q_BLOCK = 128
MACARON = 0.5
EPS = 1e-6
SCALE = 1.0 / math.sqrt(HEAD_DIM)
SPLITS = [D_SB, 2 * D_SB, 3 * D_SB, 3 * D_SB + D_FOX, 3 * D_SB + 2 * D_FOX, 3 * D_SB + 3 * D_FOX]

kernel_name = "stickbreak_forgetting_hymba_macaron_step"


def _rmsnorm(x, g):
    xf = x.astype(jnp.float32)
    y = xf * lax.rsqrt(jnp.mean(xf * xf, axis=-1, keepdims=True) + EPS)
    return (y * g.astype(jnp.float32)).astype(x.dtype)


def _head_rmsnorm(o, g):
    b, h, t, d = o.shape
    of = o.astype(jnp.float32)
    of = of * lax.rsqrt(jnp.mean(of * of, axis=-1, keepdims=True) + EPS)
    of = of.transpose(0, 2, 1, 3).reshape(b, t, h * d)
    return (of * g.astype(jnp.float32)).astype(o.dtype)


def _swiglu(x, w_gate, w_up, w_down):
    return (jax.nn.silu(x @ w_gate) * (x @ w_up)) @ w_down


def _stick_breaking_block(q_pos, q, k, v, k_pos):
    z = jnp.einsum("bhqd,bhkd->bhqk", q, k, preferred_element_type=jnp.float32) * SCALE
    visible = k_pos[None, :] < q_pos[:, None]
    sp = jax.nn.softplus(z)
    sp_vis = jnp.where(visible, sp, 0.0)
    between = lax.cumsum(sp_vis, axis=3, reverse=True) - sp_vis
    a = jnp.where(visible, jnp.exp(z - sp - between), 0.0)
    out = jnp.einsum("bhqk,bhkd->bhqd", a.astype(v.dtype), v, preferred_element_type=jnp.float32)
    return out.astype(v.dtype)


def _forgetting_block(q_pos, q, c_q, k, v, c_k, k_pos):
    z = jnp.einsum("bhqd,bhkd->bhqk", q, k, preferred_element_type=jnp.float32) * SCALE
    z = z + c_q[..., :, None] - c_k[..., None, :]
    z = jnp.where(k_pos[None, :] <= q_pos[:, None], z, -jnp.inf)
    p = jax.nn.softmax(z, axis=-1)
    out = jnp.einsum("bhqk,bhkd->bhqd", p.astype(v.dtype), v, preferred_element_type=jnp.float32)
    return out.astype(v.dtype)


def _sweep(fn, q_pos, *q_args):
    t_q = q_pos.shape[0]
    if t_q <= Q_BLOCK:
        return fn(q_pos, *q_args)
    n_blk = t_q // Q_BLOCK

    def one(i):
        s = i * Q_BLOCK
        return fn(lax.dynamic_slice_in_dim(q_pos, s, Q_BLOCK, 0),
                  *[lax.dynamic_slice_in_dim(a, s, Q_BLOCK, 2) for a in q_args])

    out = jnp.moveaxis(lax.map(one, jnp.arange(n_blk)), 0, 2)
    return out.reshape(out.shape[:2] + (t_q,) + out.shape[4:])


def _parallel_heads(hn, w_in, b_forget, g_out_sb, g_out_fox, w_out,
                    sb_k_past, sb_v_past, fox_k_past, fox_v_past, fox_lf_past):
    b, t, _ = hn.shape
    past = sb_k_past.shape[2]
    proj = hn @ w_in
    q_sb, k_sb, v_sb, q_fx, k_fx, v_fx, f_fx = jnp.split(proj, SPLITS, axis=-1)
    heads = lambda a, h: a.reshape(b, t, h, HEAD_DIM).transpose(0, 2, 1, 3)
    q_sb, k_sb, v_sb = heads(q_sb, N_HEADS_SB), heads(k_sb, N_HEADS_SB), heads(v_sb, N_HEADS_SB)
    q_fx, k_fx, v_fx = heads(q_fx, N_HEADS_FOX), heads(k_fx, N_HEADS_FOX), heads(v_fx, N_HEADS_FOX)
    lf = jax.nn.log_sigmoid((f_fx + b_forget).astype(jnp.float32)).transpose(0, 2, 1)

    q_pos = past + jnp.arange(t)
    k_pos = jnp.arange(past + t)
    k_sb_all = jnp.concatenate([sb_k_past, k_sb], axis=2)
    v_sb_all = jnp.concatenate([sb_v_past, v_sb], axis=2)
    k_fx_all = jnp.concatenate([fox_k_past, k_fx], axis=2)
    v_fx_all = jnp.concatenate([fox_v_past, v_fx], axis=2)
    c_all = jnp.cumsum(jnp.concatenate([fox_lf_past.astype(jnp.float32), lf], axis=-1), axis=-1)

    o_sb = _sweep(lambda qp, q: _stick_breaking_block(qp, q, k_sb_all, v_sb_all, k_pos), q_pos, q_sb)
    o_fx = _sweep(lambda qp, q, cq: _forgetting_block(qp, q, cq, k_fx_all, v_fx_all, c_all, k_pos),
                  q_pos, q_fx, c_all[..., past:])
    o = jnp.concatenate([_head_rmsnorm(o_sb, g_out_sb), _head_rmsnorm(o_fx, g_out_fox)], axis=-1)
    return o @ w_out, (k_sb, v_sb, k_fx, v_fx, lf)


def _layer(x, p_i, past, g_ffn1, w_ffn1_gate, w_ffn1_up, w_ffn1_down, g_mix, w_in, b_forget,
           g_out_sb, g_out_fox, w_out, g_ffn2, w_ffn2_gate, w_ffn2_up, w_ffn2_down,
           g_ple, w_ple_gate, w_ple_proj):
    x = x + MACARON * _swiglu(_rmsnorm(x, g_ffn1), w_ffn1_gate, w_ffn1_up, w_ffn1_down)
    mix, state = _parallel_heads(_rmsnorm(x, g_mix), w_in, b_forget, g_out_sb, g_out_fox, w_out, *past)
    x = x + mix
    x = x + MACARON * _swiglu(_rmsnorm(x, g_ffn2), w_ffn2_gate, w_ffn2_up, w_ffn2_down)
    gate = jax.nn.sigmoid((_rmsnorm(x, g_ple) @ w_ple_gate).astype(jnp.float32))
    x = x + (gate * (p_i @ w_ple_proj).astype(jnp.float32)).astype(x.dtype)
    return x, state


def setup_inputs(seed: int = 0) -> dict:
    key = jax.random.key(seed)
    ks = jax.random.split(key, 32)
    nrm = lambda k, shape, scale=1.0: scale * jax.random.normal(k, shape, jnp.float32)
    gain = lambda k, shape: 1.0 + 0.02 * jax.random.normal(k, shape, jnp.float32)
    sb_cache = (DEPTH, DEC_BATCH, N_HEADS_SB, PAST_LEN, HEAD_DIM)
    fx_cache = (DEPTH, DEC_BATCH, N_HEADS_FOX, PAST_LEN, HEAD_DIM)
    return {
        "x_prompt": nrm(ks[0], (BATCH, SEQ, D_MODEL)),
        "x_sample": nrm(ks[1], (DEC_BATCH, DEC_SEQ, D_MODEL)),
        "cache_sb_k": nrm(ks[2], sb_cache),
        "cache_sb_v": nrm(ks[3], sb_cache),
        "cache_fox_k": nrm(ks[4], fx_cache),
        "cache_fox_v": nrm(ks[5], fx_cache),
        "cache_fox_logf": jax.nn.log_sigmoid(2.0 + nrm(ks[6], (DEPTH, DEC_BATCH, N_HEADS_FOX, PAST_LEN))),
        "p_prompt": nrm(ks[7], (DEPTH, BATCH, SEQ, PLE_DIM)),
        "p_sample": nrm(ks[8], (DEPTH, DEC_BATCH, DEC_SEQ, PLE_DIM)),
        "g_ffn1": gain(ks[9], (DEPTH, D_MODEL)),
        "w_ffn1_gate": nrm(ks[10], (DEPTH, D_MODEL, D_FF), D_MODEL ** -0.5),
        "w_ffn1_up": nrm(ks[11], (DEPTH, D_MODEL, D_FF), D_MODEL ** -0.5),
        "w_ffn1_down": nrm(ks[12], (DEPTH, D_FF, D_MODEL), D_FF ** -0.5),
        "g_mix": gain(ks[13], (DEPTH, D_MODEL)),
        "w_in": nrm(ks[14], (DEPTH, D_MODEL, N_IN), D_MODEL ** -0.5),
        "b_forget": 2.0 + nrm(ks[15], (DEPTH, N_HEADS_FOX), 0.5),
        "g_out_sb": gain(ks[16], (DEPTH, D_SB)),
        "g_out_fox": gain(ks[17], (DEPTH, D_FOX)),
        "w_out": nrm(ks[18], (DEPTH, D_MIX, D_MODEL), D_MIX ** -0.5),
        "g_ffn2": gain(ks[19], (DEPTH, D_MODEL)),
        "w_ffn2_gate": nrm(ks[20], (DEPTH, D_MODEL, D_FF), D_MODEL ** -0.5),
        "w_ffn2_up": nrm(ks[21], (DEPTH, D_MODEL, D_FF), D_MODEL ** -0.5),
        "w_ffn2_down": nrm(ks[22], (DEPTH, D_FF, D_MODEL), D_FF ** -0.5),
        "g_ple": gain(ks[23], (DEPTH, D_MODEL)),
        "w_ple_gate": nrm(ks[24], (DEPTH, D_MODEL, D_MODEL), D_MODEL ** -0.5),
        "w_ple_proj": nrm(ks[25], (DEPTH, PLE_DIM, D_MODEL), PLE_DIM ** -0.5),
        "g_final": gain(ks[26], (D_MODEL,)),
    }


def reference(x_prompt, x_sample, cache_sb_k, cache_sb_v, cache_fox_k, cache_fox_v, cache_fox_logf,
              p_prompt, p_sample, g_ffn1, w_ffn1_gate, w_ffn1_up, w_ffn1_down, g_mix, w_in, b_forget,
              g_out_sb, g_out_fox, w_out, g_ffn2, w_ffn2_gate, w_ffn2_up, w_ffn2_down,
              g_ple, w_ple_gate, w_ple_proj, g_final):
    def run(x, p, past_of_layer):
        states = []
        for i in range(DEPTH):
            x, st = _layer(x, p[i], past_of_layer(i), g_ffn1[i], w_ffn1_gate[i], w_ffn1_up[i], w_ffn1_down[i],
                           g_mix[i], w_in[i], b_forget[i], g_out_sb[i], g_out_fox[i], w_out[i],
                           g_ffn2[i], w_ffn2_gate[i], w_ffn2_up[i], w_ffn2_down[i],
                           g_ple[i], w_ple_gate[i], w_ple_proj[i])
            states.append(st)
        sb_k, sb_v, fx_k, fx_v, fx_lf = [jnp.stack(s, axis=0) for s in zip(*states)]
        return _rmsnorm(x, g_final), sb_k, sb_v, fx_k, fx_v, fx_lf

    b, dt = x_prompt.shape[0], x_prompt.dtype
    empty_past = lambda i: (jnp.zeros((b, N_HEADS_SB, 0, HEAD_DIM), dt), jnp.zeros((b, N_HEADS_SB, 0, HEAD_DIM), dt),
                            jnp.zeros((b, N_HEADS_FOX, 0, HEAD_DIM), dt), jnp.zeros((b, N_HEADS_FOX, 0, HEAD_DIM), dt),
                            jnp.zeros((b, N_HEADS_FOX, 0), jnp.float32))
    cache_past = lambda i: (cache_sb_k[i], cache_sb_v[i], cache_fox_k[i], cache_fox_v[i], cache_fox_logf[i])

    y_prompt, sb_k_p, sb_v_p, fx_k_p, fx_v_p, fx_lf_p = run(x_prompt, p_prompt, empty_past)
    y_sample, sb_k_s, sb_v_s, fx_k_s, fx_v_s, fx_lf_s = run(x_sample, p_sample, cache_past)
    return (y_prompt, y_sample, sb_k_p, sb_v_p, fx_k_p, fx_v_p, fx_lf_p, sb_k_s, sb_v_s, fx_k_s, fx_v_s, fx_lf_s)
```

```python
import functools
import math

import jax
import jax.numpy as jnp
from jax import lax
from jax.experimental import pallas as pl
from jax.experimental.pallas import tpu as pltpu

HEAD_DIM = 128
N_HEADS = 8
D_GROUP = N_HEADS * HEAD_DIM
MACARON = 0.5
EPS = 1e-6
SCALE = 1.0 / math.sqrt(HEAD_DIM)
LANES = 128
V7X_VMEM_LIMIT = 56 * 1024 * 1024

F32 = jnp.float32
BF16 = jnp.bfloat16


def _params(semantics):
    return pltpu.CompilerParams(dimension_semantics=semantics, vmem_limit_bytes=V7X_VMEM_LIMIT)


def _rms(x, g):
    return x * lax.rsqrt(jnp.mean(x * x, axis=-1, keepdims=True) + EPS) * g


def _dot(a, b):
    return jnp.dot(a, b, preferred_element_type=F32)


def _dot_nt(a, b):
    return lax.dot_general(a, b, (((1,), (1,)), ((), ())), preferred_element_type=F32)


def _split3(x):
    hi = x.astype(BF16)
    r = x - hi.astype(F32)
    mid = r.astype(BF16)
    lo = (r - mid.astype(F32)).astype(BF16)
    return hi, mid, lo


def _split2(x):
    hi = x.astype(BF16)
    lo = (x - hi.astype(F32)).astype(BF16)
    return hi, lo


def _ffn_kernel(x_ref, g_ref, wg_ref, wu_ref, wd_ref, gn_ref, o_ref, on_ref, xn_ref):
    f = pl.program_id(1)

    @pl.when(f == 0)
    def _():
        xn_ref[...] = _rms(x_ref[...], g_ref[...]).astype(BF16)
        o_ref[...] = jnp.zeros_like(o_ref)

    xn = xn_ref[...]
    gate = _dot(xn, wg_ref[...])
    up = _dot(xn, wu_ref[...])
    h = (gate * jax.nn.sigmoid(gate) * up).astype(BF16)
    o_ref[...] += _dot(h, wd_ref[...])

    @pl.when(f == pl.num_programs(1) - 1)
    def _():
        y = x_ref[...] + MACARON * o_ref[...]
        o_ref[...] = y
        on_ref[...] = _rms(y, gn_ref[...]).astype(BF16)


def _ffn(x, g, wg, wu, wd, g_next, tm, tf):
    m, d = x.shape
    d_ff = wg.shape[1]
    row = lambda i, f: (i, 0)
    return pl.pallas_call(
        _ffn_kernel,
        grid=(m // tm, d_ff // tf),
        in_specs=[
            pl.BlockSpec((tm, d), row),
            pl.BlockSpec((1, d), lambda i, f: (0, 0)),
            pl.BlockSpec((d, tf), lambda i, f: (0, f)),
            pl.BlockSpec((d, tf), lambda i, f: (0, f)),
            pl.BlockSpec((tf, d), lambda i, f: (f, 0)),
            pl.BlockSpec((1, d), lambda i, f: (0, 0)),
        ],
        out_specs=[pl.BlockSpec((tm, d), row), pl.BlockSpec((tm, d), row)],
        out_shape=[jax.ShapeDtypeStruct((m, d), F32), jax.ShapeDtypeStruct((m, d), BF16)],
        scratch_shapes=[pltpu.VMEM((tm, d), BF16)],
        compiler_params=_params(("parallel", "arbitrary")),
        name="ffn",
    )(x, g, wg, wu, wd, g_next)


def _proj_kernel(xn_ref, w_ref, wf_ref, bf_ref, qkv_ref, ksb_ref, vsb_ref, kfx_ref, vfx_ref, lf_ref):
    n = pl.program_id(1)
    xn = xn_ref[...]
    res = _dot(xn, w_ref[...])
    is_q = jnp.logical_or(n == 0, n == 3)
    res_mx = res * jnp.where(is_q, SCALE, 1.0)
    for h in range(N_HEADS):
        qkv_ref[0, h] = res_mx[:, h * HEAD_DIM:(h + 1) * HEAD_DIM].astype(BF16)

    def store_heads(ref):
        for h in range(N_HEADS):
            ref[h] = res[:, h * HEAD_DIM:(h + 1) * HEAD_DIM]

    for idx, ref in ((1, ksb_ref), (2, vsb_ref), (4, kfx_ref), (5, vfx_ref)):
        pl.when(n == idx)(functools.partial(store_heads, ref))

    @pl.when(n == 0)
    def _():
        fz = _dot(xn, wf_ref[...]) + bf_ref[...]
        lf_ref[...] = jnp.minimum(fz, 0.0) - jnp.log1p(jnp.exp(-jnp.abs(fz)))


def _proj(xn, w_qkv, w_f, b_f, tm):
    m, d = xn.shape
    head_f32 = jax.ShapeDtypeStruct((N_HEADS, m, HEAD_DIM), F32)
    head_spec = pl.BlockSpec((N_HEADS, tm, HEAD_DIM), lambda i, n: (0, i, 0))
    return pl.pallas_call(
        _proj_kernel,
        grid=(m // tm, 6),
        in_specs=[
            pl.BlockSpec((tm, d), lambda i, n: (i, 0)),
            pl.BlockSpec((d, D_GROUP), lambda i, n: (0, n)),
            pl.BlockSpec((d, LANES), lambda i, n: (0, 0)),
            pl.BlockSpec((1, LANES), lambda i, n: (0, 0)),
        ],
        out_specs=[
            pl.BlockSpec((1, N_HEADS, tm, HEAD_DIM), lambda i, n: (n, 0, i, 0)),
            head_spec, head_spec, head_spec, head_spec,
            pl.BlockSpec((tm, LANES), lambda i, n: (i, 0)),
        ],
        out_shape=[
            jax.ShapeDtypeStruct((6, N_HEADS, m, HEAD_DIM), BF16),
            head_f32, head_f32, head_f32, head_f32,
            jax.ShapeDtypeStruct((m, LANES), F32),
        ],
        compiler_params=_params(("parallel", "arbitrary")),
        name="in_proj",
    )(xn, w_qkv, w_f, b_f)


def _cumsum_kernel(x_ref, o_ref):
    n_chunks, rows, _ = x_ref.shape
    upper = (lax.broadcasted_iota(jnp.int32, (LANES, LANES), 0)
             <= lax.broadcasted_iota(jnp.int32, (LANES, LANES), 1)).astype(BF16)

    def body(c, carry):
        hi, mid, lo = _split3(x_ref[c])
        cs = _dot(hi, upper) + _dot(mid, upper) + _dot(lo, upper) + carry
        o_ref[c] = cs
        return cs[:, LANES - 1:LANES]

    lax.fori_loop(0, n_chunks, body, jnp.zeros((rows, 1), F32))


def _cumsum_rows(x):
    rows, length = x.shape
    xc = x.reshape(rows, length // LANES, LANES).transpose(1, 0, 2)
    out = pl.pallas_call(
        _cumsum_kernel,
        out_shape=jax.ShapeDtypeStruct(xc.shape, F32),
        compiler_params=pltpu.CompilerParams(vmem_limit_bytes=V7X_VMEM_LIMIT),
        name="cumsum",
    )(xc)
    return out.transpose(1, 0, 2).reshape(rows, length)


def _strict_lower(n):
    return (lax.broadcasted_iota(jnp.int32, (n, n), 0)
            >= lax.broadcasted_iota(jnp.int32, (n, n), 1)).astype(BF16)


def _sb_block(q, k, v, suffix, carry, acc, visible):
    z = _dot_nt(q, k)
    sp = jnp.maximum(z, 0.0) + jnp.log(1.0 + jnp.exp(-jnp.abs(z)))
    if visible is not None:
        sp = jnp.where(visible, sp, 0.0)
    hi, lo = _split2(sp)
    incl = _dot(hi, suffix) + _dot(lo, suffix)
    a = jnp.exp(z - incl - carry)
    if visible is not None:
        a = jnp.where(visible, a, 0.0)
    acc = acc + _dot(a.astype(BF16), v)
    carry = carry + jnp.sum(sp, axis=-1, keepdims=True)
    return carry, acc


def _fox_block(q, k, v, bias, m, l, acc, visible):
    s = _dot_nt(q, k) + bias
    if visible is not None:
        s = jnp.where(visible, s, -jnp.inf)
    m_new = jnp.maximum(m, jnp.max(s, axis=-1, keepdims=True))
    alpha = jnp.exp(m - m_new)
    p = jnp.exp(s - m_new)
    l = alpha * l + jnp.sum(p, axis=-1, keepdims=True)
    acc = alpha * acc + _dot(p.astype(BF16), v)
    return m_new, l, acc


def _head_norm(o, g):
    return (o * lax.rsqrt(jnp.mean(o * o, axis=-1, keepdims=True) + EPS) * g).astype(BF16)


def _sb_prompt_kernel(q_ref, k_ref, v_ref, g_ref, o_ref, *, blk):
    qi = pl.program_id(1)
    q = q_ref[0, 0]
    suffix = _strict_lower(blk)
    rows = lax.broadcasted_iota(jnp.int32, (blk, blk), 0)
    cols = lax.broadcasted_iota(jnp.int32, (blk, blk), 1)

    def kv(j):
        start = pl.multiple_of(j * blk, blk)
        return k_ref[0, 0, pl.ds(start, blk), :], v_ref[0, 0, pl.ds(start, blk), :]

    k, v = kv(qi)
    carry, acc = _sb_block(q, k, v, suffix, jnp.zeros((blk, 1), F32),
                           jnp.zeros((blk, HEAD_DIM), F32), cols < rows)

    def body(step, state):
        k, v = kv(qi - 1 - step)
        return _sb_block(q, k, v, suffix, state[0], state[1], None)

    carry, acc = lax.fori_loop(0, qi, body, (carry, acc))
    o_ref[...] = _head_norm(acc, g_ref[0])


def _fox_prompt_kernel(q_ref, k_ref, v_ref, cq_ref, ck_ref, g_ref, o_ref, *, blk):
    qi = pl.program_id(1)
    q = q_ref[0, 0]
    cq = cq_ref[0]
    rows = lax.broadcasted_iota(jnp.int32, (blk, blk), 0)
    cols = lax.broadcasted_iota(jnp.int32, (blk, blk), 1)

    def blk_inputs(j):
        start = pl.multiple_of(j * blk, blk)
        return (k_ref[0, 0, pl.ds(start, blk), :], v_ref[0, 0, pl.ds(start, blk), :],
                cq - ck_ref[0, j])

    def body(j, state):
        k, v, bias = blk_inputs(j)
        return _fox_block(q, k, v, bias, *state, None)

    init = (jnp.full((blk, 1), -jnp.inf, F32), jnp.zeros((blk, 1), F32), jnp.zeros((blk, HEAD_DIM), F32))
    state = lax.fori_loop(0, qi, body, init)
    k, v, bias = blk_inputs(qi)
    _, l, acc = _fox_block(q, k, v, bias, *state, cols <= rows)
    o_ref[...] = _head_norm(acc / l, g_ref[0])


def _prompt_attention(qkv, c_rows, g_sb, g_fx, blk):
    t = qkv.shape[2]
    nq = t // blk
    q_spec = lambda n: pl.BlockSpec((1, 1, blk, HEAD_DIM), lambda h, i: (n, h, i, 0))
    kv_spec = lambda n: pl.BlockSpec((1, 1, t, HEAD_DIM), lambda h, i: (n, h, 0, 0))
    g_spec = pl.BlockSpec((1, 1, HEAD_DIM), lambda h, i: (h, 0, 0))
    o_spec = pl.BlockSpec((blk, HEAD_DIM), lambda h, i: (i, h))
    o_shape = jax.ShapeDtypeStruct((t, D_GROUP), BF16)
    o_sb = pl.pallas_call(
        functools.partial(_sb_prompt_kernel, blk=blk),
        grid=(N_HEADS, nq),
        in_specs=[q_spec(0), kv_spec(1), kv_spec(2), g_spec],
        out_specs=o_spec, out_shape=o_shape,
        compiler_params=_params(("parallel", "arbitrary")),
        name="sb_prompt",
    )(qkv, qkv, qkv, g_sb)
    cq = c_rows.reshape(N_HEADS, t, 1)
    ck = c_rows.reshape(N_HEADS, nq, 1, blk)
    o_fx = pl.pallas_call(
        functools.partial(_fox_prompt_kernel, blk=blk),
        grid=(N_HEADS, nq),
        in_specs=[q_spec(3), kv_spec(4), kv_spec(5),
                  pl.BlockSpec((1, blk, 1), lambda h, i: (h, i, 0)),
                  pl.BlockSpec((1, nq, 1, blk), lambda h, i: (h, 0, 0, 0)),
                  g_spec],
        out_specs=o_spec, out_shape=o_shape,
        compiler_params=_params(("parallel", "arbitrary")),
        name="fox_prompt",
    )(qkv, qkv, qkv, cq, ck, g_fx)
    return o_sb, o_fx


def _sb_sample_kernel(q_ref, k_ref, v_ref, pk_ref, pv_ref, g_ref, o_ref, *, blk):
    tq = q_ref.shape[3]
    past = pk_ref.shape[3]
    q = q_ref[0, 0, 0]
    rows = lax.broadcasted_iota(jnp.int32, (tq, tq), 0)
    cols = lax.broadcasted_iota(jnp.int32, (tq, tq), 1)
    carry, acc = _sb_block(q, k_ref[0, 0, 0], v_ref[0, 0, 0], _strict_lower(tq),
                           jnp.zeros((tq, 1), F32), jnp.zeros((tq, HEAD_DIM), F32), cols < rows)
    suffix = _strict_lower(blk)
    n_blk = past // blk

    def body(step, state):
        start = pl.multiple_of((n_blk - 1 - step) * blk, blk)
        k = pk_ref[0, 0, 0, pl.ds(start, blk), :].astype(BF16)
        v = pv_ref[0, 0, 0, pl.ds(start, blk), :].astype(BF16)
        return _sb_block(q, k, v, suffix, state[0], state[1], None)

    carry, acc = lax.fori_loop(0, n_blk, body, (carry, acc))
    o_ref[...] = _head_norm(acc, g_ref[0])


def _fox_sample_kernel(q_ref, k_ref, v_ref, pk_ref, pv_ref, cq_ref, cn_ref, cp_ref, g_ref, o_ref, *, blk):
    tq = q_ref.shape[3]
    past = pk_ref.shape[3]
    q = q_ref[0, 0, 0]
    cq = cq_ref[0, 0]
    n_blk = past // blk

    def body(j, state):
        start = pl.multiple_of(j * blk, blk)
        k = pk_ref[0, 0, 0, pl.ds(start, blk), :].astype(BF16)
        v = pv_ref[0, 0, 0, pl.ds(start, blk), :].astype(BF16)
        return _fox_block(q, k, v, cq - cp_ref[0, 0, j], *state, None)

    init = (jnp.full((tq, 1), -jnp.inf, F32), jnp.zeros((tq, 1), F32), jnp.zeros((tq, HEAD_DIM), F32))
    state = lax.fori_loop(0, n_blk, body, init)
    rows = lax.broadcasted_iota(jnp.int32, (tq, tq), 0)
    cols = lax.broadcasted_iota(jnp.int32, (tq, tq), 1)
    _, l, acc = _fox_block(q, k_ref[0, 0, 0], v_ref[0, 0, 0], cq - cn_ref[0, 0], *state, cols <= rows)
    o_ref[...] = _head_norm(acc / l, g_ref[0])


def _sample_attention(qkv, past_sb_k, past_sb_v, past_fx_k, past_fx_v, c_all, g_sb, g_fx, blk):
    b, _, past, _ = past_sb_k.shape
    tq = qkv.shape[2] // b
    qkv = qkv.reshape(6, N_HEADS, b, tq, HEAD_DIM)
    new_spec = lambda n: pl.BlockSpec((1, 1, 1, tq, HEAD_DIM), lambda bi, h: (n, h, bi, 0, 0))
    past_spec = pl.BlockSpec((1, 1, 1, past, HEAD_DIM), lambda bi, h: (0, bi, h, 0, 0))
    g_spec = pl.BlockSpec((1, 1, HEAD_DIM), lambda bi, h: (h, 0, 0))
    o_spec = pl.BlockSpec((tq, HEAD_DIM), lambda bi, h: (bi, h))
    o_shape = jax.ShapeDtypeStruct((b * tq, D_GROUP), BF16)
    o_sb = pl.pallas_call(
        functools.partial(_sb_sample_kernel, blk=blk),
        grid=(b, N_HEADS),
        in_specs=[new_spec(0), new_spec(1), new_spec(2), past_spec, past_spec, g_spec],
        out_specs=o_spec, out_shape=o_shape,
        compiler_params=_params(("parallel", "parallel")),
        name="sb_sample",
    )(qkv, qkv, qkv, past_sb_k[None], past_sb_v[None], g_sb)
    n_blk = past // blk
    cq = c_all[:, :, past:].reshape(b, N_HEADS, tq, 1)
    cn = c_all[:, :, past:].reshape(b, N_HEADS, 1, tq)
    cp = c_all[:, :, :past].reshape(b, N_HEADS, n_blk, 1, blk)
    o_fx = pl.pallas_call(
        functools.partial(_fox_sample_kernel, blk=blk),
        grid=(b, N_HEADS),
        in_specs=[new_spec(3), new_spec(4), new_spec(5), past_spec, past_spec,
                  pl.BlockSpec((1, 1, tq, 1), lambda bi, h: (bi, h, 0, 0)),
                  pl.BlockSpec((1, 1, 1, tq), lambda bi, h: (bi, h, 0, 0)),
                  pl.BlockSpec((1, 1, n_blk, 1, blk), lambda bi, h: (bi, h, 0, 0, 0)),
                  g_spec],
        out_specs=o_spec, out_shape=o_shape,
        compiler_params=_params(("parallel", "parallel")),
        name="fox_sample",
    )(qkv, qkv, qkv, past_fx_k[None], past_fx_v[None], cq, cn, cp, g_fx)
    return o_sb, o_fx


def _out_proj_kernel(x_ref, osb_ref, ofx_ref, wsb_ref, wfx_ref, o_ref):
    o_ref[...] = x_ref[...] + _dot(osb_ref[...], wsb_ref[...]) + _dot(ofx_ref[...], wfx_ref[...])


def _out_proj(x, o_sb, o_fx, w_sb, w_fx, tm):
    m, d = x.shape
    row = lambda i: (i, 0)
    full = lambda i: (0, 0)
    return pl.pallas_call(
        _out_proj_kernel,
        grid=(m // tm,),
        in_specs=[pl.BlockSpec((tm, d), row), pl.BlockSpec((tm, D_GROUP), row), pl.BlockSpec((tm, D_GROUP), row),
                  pl.BlockSpec((D_GROUP, d), full), pl.BlockSpec((D_GROUP, d), full)],
        out_specs=pl.BlockSpec((tm, d), row),
        out_shape=jax.ShapeDtypeStruct((m, d), F32),
        compiler_params=_params(("parallel",)),
        name="out_proj",
    )(x, o_sb, o_fx, w_sb, w_fx)


def _ple_kernel(x_ref, xn_ref, p_ref, wg_ref, wp_ref, gf_ref, o_ref):
    gate = jax.nn.sigmoid(_dot(xn_ref[...], wg_ref[...]))
    emb = _dot(p_ref[...].astype(BF16), wp_ref[...])
    o_ref[...] = _rms(x_ref[...] + gate * emb, gf_ref[...])


def _ple_final(x, xn, p, w_gate, w_proj, g_final, tm):
    m, d = x.shape
    ple = p.shape[1]
    row = lambda i: (i, 0)
    full = lambda i: (0, 0)
    return pl.pallas_call(
        _ple_kernel,
        grid=(m // tm,),
        in_specs=[pl.BlockSpec((tm, d), row), pl.BlockSpec((tm, d), row), pl.BlockSpec((tm, ple), row),
                  pl.BlockSpec((d, d), full), pl.BlockSpec((ple, d), full), pl.BlockSpec((1, d), full)],
        out_specs=pl.BlockSpec((tm, d), row),
        out_shape=jax.ShapeDtypeStruct((m, d), F32),
        compiler_params=_params(("parallel",)),
        name="ple_final",
    )(x, xn, p, w_gate, w_proj, g_final)


def _pad_lanes(x, n):
    return jnp.pad(x, ((0, 0), (0, n - x.shape[1])))


def _run_group(x, p, past, w, attn_blk):
    b, t, d = x.shape
    m = b * t
    tm = min(512, m)
    row = lambda g: g.reshape(1, -1)
    x0 = x.reshape(m, d)
    x1, xn = _ffn(x0, row(w["g_ffn1"]), w["ffn1_gate"], w["ffn1_up"], w["ffn1_down"], row(w["g_mix"]), tm, 512)
    qkv, k_sb, v_sb, k_fx, v_fx, lf_cols = _proj(xn, w["in_qkv"], w["in_f"], w["b_f"], tm)
    lf = lf_cols[:, :N_HEADS].reshape(b, t, N_HEADS).transpose(0, 2, 1)
    g_sb = w["g_out_sb"].reshape(N_HEADS, 1, HEAD_DIM)
    g_fx = w["g_out_fox"].reshape(N_HEADS, 1, HEAD_DIM)
    if past is None:
        c_rows = _cumsum_rows(lf.reshape(N_HEADS, t))
        o_sb, o_fx = _prompt_attention(qkv, c_rows, g_sb, g_fx, attn_blk)
    else:
        lf_all = jnp.concatenate([past[4], lf], axis=-1).reshape(b * N_HEADS, -1)
        total = lf_all.shape[1]
        padded = -(-total // LANES) * LANES
        c_all = _cumsum_rows(_pad_lanes(lf_all, padded))[:, :total].reshape(b, N_HEADS, total)
        o_sb, o_fx = _sample_attention(qkv, past[0], past[1], past[2], past[3], c_all, g_sb, g_fx, attn_blk)
    x2 = _out_proj(x1, o_sb, o_fx, w["out_sb"], w["out_fx"], tm)
    x3, xn3 = _ffn(x2, row(w["g_ffn2"]), w["ffn2_gate"], w["ffn2_up"], w["ffn2_down"], row(w["g_ple"]), tm, 512)
    y = _ple_final(x3, xn3, p.reshape(m, -1), w["ple_gate"], w["ple_proj"], row(w["g_final"]), min(256, m))
    heads = lambda a: a.reshape(N_HEADS, b, t, HEAD_DIM).transpose(1, 0, 2, 3)[None]
    return y.reshape(b, t, d), heads(k_sb), heads(v_sb), heads(k_fx), heads(v_fx), lf[None]


def kernel(x_prompt, x_sample, cache_sb_k, cache_sb_v, cache_fox_k, cache_fox_v, cache_fox_logf, p_prompt, p_sample, g_ffn1, w_ffn1_gate, w_ffn1_up, w_ffn1_down, g_mix, w_in, b_forget, g_out_sb, g_out_fox, w_out, g_ffn2, w_ffn2_gate, w_ffn2_up, w_ffn2_down, g_ple, w_ple_gate, w_ple_proj, g_final):
    assert g_ffn1.shape[0] == 1, "single-layer trunk"
    bf = lambda a: a.astype(BF16)
    n_qkv = 6 * D_GROUP
    w = {
        "g_ffn1": g_ffn1[0], "ffn1_gate": bf(w_ffn1_gate[0]), "ffn1_up": bf(w_ffn1_up[0]), "ffn1_down": bf(w_ffn1_down[0]),
        "g_mix": g_mix[0], "in_qkv": bf(w_in[0][:, :n_qkv]),
        "in_f": _pad_lanes(bf(w_in[0][:, n_qkv:]), LANES), "b_f": _pad_lanes(b_forget[0][None, :], LANES),
        "g_out_sb": g_out_sb[0], "g_out_fox": g_out_fox[0],
        "out_sb": bf(w_out[0][:D_GROUP]), "out_fx": bf(w_out[0][D_GROUP:]),
        "g_ffn2": g_ffn2[0], "ffn2_gate": bf(w_ffn2_gate[0]), "ffn2_up": bf(w_ffn2_up[0]), "ffn2_down": bf(w_ffn2_down[0]),
        "g_ple": g_ple[0], "ple_gate": bf(w_ple_gate[0]), "ple_proj": bf(w_ple_proj[0]), "g_final": g_final,
    }
    past = (cache_sb_k[0], cache_sb_v[0], cache_fox_k[0], cache_fox_v[0], cache_fox_logf[0])
    y_p, sb_k_p, sb_v_p, fx_k_p, fx_v_p, lf_p = _run_group(x_prompt, p_prompt[0], None, w, 256)
    y_s, sb_k_s, sb_v_s, fx_k_s, fx_v_s, lf_s = _run_group(x_sample, p_sample[0], past, w, 256)
    return (y_p, y_s, sb_k_p, sb_v_p, fx_k_p, fx_v_p, lf_p, sb_k_s, sb_v_s, fx_k_s, fx_v_s, lf_s)
```

```python
import functools
import math

import jax
import jax.numpy as jnp
from jax import lax
from jax.experimental import pallas as pl
from jax.experimental.pallas import tpu as pltpu

HEAD_DIM = 128
N_HEADS = 8
D_GROUP = N_HEADS * HEAD_DIM
MACARON = 0.5
EPS = 1e-6
SCALE = 1.0 / math.sqrt(HEAD_DIM)
LANES = 128
V7X_VMEM_LIMIT = 56 * 1024 * 1024

F32 = jnp.float32
BF16 = jnp.bfloat16


def _params(semantics):
    return pltpu.CompilerParams(dimension_semantics=semantics, vmem_limit_bytes=V7X_VMEM_LIMIT)


def _rms(x, g):
    return x * lax.rsqrt(jnp.mean(x * x, axis=-1, keepdims=True) + EPS) * g


def _dot(a, b):
    return jnp.dot(a, b, preferred_element_type=F32)


def _dot_nt(a, b):
    return lax.dot_general(a, b, (((1,), (1,)), ((), ())), preferred_element_type=F32)


def _split3(x):
    hi = x.astype(BF16)
    r = x - hi.astype(F32)
    mid = r.astype(BF16)
    lo = (r - mid.astype(F32)).astype(BF16)
    return hi, mid, lo


def _split2(x):
    hi = x.astype(BF16)
    lo = (x - hi.astype(F32)).astype(BF16)
    return hi, lo


def _ffn_kernel(x_ref, g_ref, wg_ref, wu_ref, wd_ref, gn_ref, o_ref, on_ref, xn_ref):
    f = pl.program_id(1)

    @pl.when(f == 0)
    def _():
        xn_ref[...] = _rms(x_ref[...], g_ref[...]).astype(BF16)
        o_ref[...] = jnp.zeros_like(o_ref)

    xn = xn_ref[...]
    gate = _dot(xn, wg_ref[...])
    up = _dot(xn, wu_ref[...])
    h = (gate * jax.nn.sigmoid(gate) * up).astype(BF16)
    o_ref[...] += _dot(h, wd_ref[...])

    @pl.when(f == pl.num_programs(1) - 1)
    def _():
        y = x_ref[...] + MACARON * o_ref[...]
        o_ref[...] = y
        on_ref[...] = _rms(y, gn_ref[...]).astype(BF16)


def _ffn(x, g, wg, wu, wd, g_next, tm, tf):
    m, d = x.shape
    d_ff = wg.shape[1]
    row = lambda i, f: (i, 0)
    return pl.pallas_call(
        _ffn_kernel,
        grid=(m // tm, d_ff // tf),
        in_specs=[
            pl.BlockSpec((tm, d), row),
            pl.BlockSpec((1, d), lambda i, f: (0, 0)),
            pl.BlockSpec((d, tf), lambda i, f: (0, f)),
            pl.BlockSpec((d, tf), lambda i, f: (0, f)),
            pl.BlockSpec((tf, d), lambda i, f: (f, 0)),
            pl.BlockSpec((1, d), lambda i, f: (0, 0)),
        ],
        out_specs=[pl.BlockSpec((tm, d), row), pl.BlockSpec((tm, d), row)],
        out_shape=[jax.ShapeDtypeStruct((m, d), F32), jax.ShapeDtypeStruct((m, d), BF16)],
        scratch_shapes=[pltpu.VMEM((tm, d), BF16)],
        compiler_params=_params(("parallel", "arbitrary")),
        name="ffn",
    )(x, g, wg, wu, wd, g_next)


def _proj_kernel(xn_ref, w_ref, wf_ref, bf_ref, qkv_ref, ksb_ref, vsb_ref, kfx_ref, vfx_ref, lf_ref):
    n = pl.program_id(1)
    xn = xn_ref[...]
    res = _dot(xn, w_ref[...])
    is_q = jnp.logical_or(n == 0, n == 3)
    res_mx = res * jnp.where(is_q, SCALE, 1.0)
    for h in range(N_HEADS):
        qkv_ref[0, h] = res_mx[:, h * HEAD_DIM:(h + 1) * HEAD_DIM].astype(BF16)

    def store_heads(ref):
        for h in range(N_HEADS):
            ref[h] = res[:, h * HEAD_DIM:(h + 1) * HEAD_DIM]

    for idx, ref in ((1, ksb_ref), (2, vsb_ref), (4, kfx_ref), (5, vfx_ref)):
        pl.when(n == idx)(functools.partial(store_heads, ref))

    @pl.when(n == 0)
    def _():
        fz = _dot(xn, wf_ref[...]) + bf_ref[...]
        lf_ref[...] = jnp.minimum(fz, 0.0) - jnp.log1p(jnp.exp(-jnp.abs(fz)))


def _proj(xn, w_qkv, w_f, b_f, tm):
    m, d = xn.shape
    head_f32 = jax.ShapeDtypeStruct((N_HEADS, m, HEAD_DIM), F32)
    head_spec = pl.BlockSpec((N_HEADS, tm, HEAD_DIM), lambda i, n: (0, i, 0))
    return pl.pallas_call(
        _proj_kernel,
        grid=(m // tm, 6),
        in_specs=[
            pl.BlockSpec((tm, d), lambda i, n: (i, 0)),
            pl.BlockSpec((d, D_GROUP), lambda i, n: (0, n)),
            pl.BlockSpec((d, LANES), lambda i, n: (0, 0)),
            pl.BlockSpec((1, LANES), lambda i, n: (0, 0)),
        ],
        out_specs=[
            pl.BlockSpec((1, N_HEADS, tm, HEAD_DIM), lambda i, n: (n, 0, i, 0)),
            head_spec, head_spec, head_spec, head_spec,
            pl.BlockSpec((tm, LANES), lambda i, n: (i, 0)),
        ],
        out_shape=[
            jax.ShapeDtypeStruct((6, N_HEADS, m, HEAD_DIM), BF16),
            head_f32, head_f32, head_f32, head_f32,
            jax.ShapeDtypeStruct((m, LANES), F32),
        ],
        compiler_params=_params(("parallel", "arbitrary")),
        name="in_proj",
    )(xn, w_qkv, w_f, b_f)


def _cumsum_kernel(x_ref, o_ref):
    n_chunks, rows, _ = x_ref.shape
    upper = (lax.broadcasted_iota(jnp.int32, (LANES, LANES), 0)
             <= lax.broadcasted_iota(jnp.int32, (LANES, LANES), 1)).astype(BF16)

    def body(c, carry):
        hi, mid, lo = _split3(x_ref[c])
        cs = _dot(hi, upper) + _dot(mid, upper) + _dot(lo, upper) + carry
        o_ref[c] = cs
        return cs[:, LANES - 1:LANES]

    lax.fori_loop(0, n_chunks, body, jnp.zeros((rows, 1), F32))


def _cumsum_rows(x):
    rows, length = x.shape
    xc = x.reshape(rows, length // LANES, LANES).transpose(1, 0, 2)
    out = pl.pallas_call(
        _cumsum_kernel,
        out_shape=jax.ShapeDtypeStruct(xc.shape, F32),
        compiler_params=pltpu.CompilerParams(vmem_limit_bytes=V7X_VMEM_LIMIT),
        name="cumsum",
    )(xc)
    return out.transpose(1, 0, 2).reshape(rows, length)


def _suffix_ones(n):
    return (lax.broadcasted_iota(jnp.int32, (n, n), 0)
            >= lax.broadcasted_iota(jnp.int32, (n, n), 1)).astype(BF16)


def _sb_step(qs, k, v, suffix, states, masks):
    live = [c for c, m in enumerate(masks) if not isinstance(m, str)]
    z = {c: _dot_nt(qs[c], k) for c in live}
    sp, parts = {}, {}
    for c in live:
        s = jnp.maximum(z[c], 0.0) + jnp.log(1.0 + jnp.exp(-jnp.abs(z[c])))
        sp[c] = s if masks[c] is None else jnp.where(masks[c], s, 0.0)
        parts[c] = _split2(sp[c])
    incl = {c: _dot(parts[c][0], suffix) + _dot(parts[c][1], suffix) for c in live}
    out = list(states)
    for c in live:
        carry, acc = states[c]
        a = jnp.exp(z[c] - incl[c] - carry)
        if masks[c] is not None:
            a = jnp.where(masks[c], a, 0.0)
        out[c] = (carry + jnp.sum(sp[c], axis=-1, keepdims=True), acc + _dot(a.astype(BF16), v))
    return out


def _fox_step(qs, k, v, biases, states, masks):
    live = [c for c, m in enumerate(masks) if not isinstance(m, str)]
    s = {}
    for c in live:
        sc = _dot_nt(qs[c], k) + biases[c]
        s[c] = sc if masks[c] is None else jnp.where(masks[c], sc, -jnp.inf)
    out = list(states)
    for c in live:
        m, l, acc = states[c]
        m_new = jnp.maximum(m, jnp.max(s[c], axis=-1, keepdims=True))
        alpha = jnp.exp(m - m_new)
        p = jnp.exp(s[c] - m_new)
        out[c] = (m_new, alpha * l + jnp.sum(p, axis=-1, keepdims=True), alpha * acc + _dot(p.astype(BF16), v))
    return out


def _diag_masks(jj, tk, rc, n_chains, strict):
    masks = []
    for c in range(n_chains):
        lo_row, hi_row = c * rc, (c + 1) * rc - 1
        lo_key, hi_key = jj * tk, (jj + 1) * tk - 1
        if (lo_key >= hi_row) if strict else (lo_key > hi_row):
            masks.append("skip")
        elif (hi_key < lo_row) if strict else (hi_key <= lo_row):
            masks.append(None)
        else:
            rows = lax.broadcasted_iota(jnp.int32, (rc, tk), 0) + lo_row
            cols = lax.broadcasted_iota(jnp.int32, (rc, tk), 1) + lo_key
            masks.append(cols < rows if strict else cols <= rows)
    return masks


def _head_norm(o, g):
    return (o * lax.rsqrt(jnp.mean(o * o, axis=-1, keepdims=True) + EPS) * g).astype(BF16)


def _sb_prompt_kernel(q_ref, k_ref, v_ref, g_ref, o_ref, z_ref, a_ref, acc_ref, *, tq, tk, n_chains):
    qi = pl.program_id(1)
    rc = tq // n_chains
    ratio = tq // tk
    qs = [q_ref[0, 0, c * rc:(c + 1) * rc, :] for c in range(n_chains)]
    suffix = _suffix_ones(tk)

    def kv(j):
        start = pl.multiple_of(j * tk, tk)
        return k_ref[0, 0, pl.ds(start, tk), :], v_ref[0, 0, pl.ds(start, tk), :]

    states = [(jnp.zeros((rc, 1), F32), jnp.zeros((rc, HEAD_DIM), F32)) for _ in range(n_chains)]
    for jj in reversed(range(ratio)):
        k, v = kv(qi * ratio + jj)
        states = _sb_step(qs, k, v, suffix, states, _diag_masks(jj, tk, rc, n_chains, True))

    n_main = qi * ratio
    chains = range(n_chains)

    def put_scores(slot, j):
        k, _ = kv(jnp.maximum(j, 0))
        for c in chains:
            z_ref[slot, c] = _dot_nt(qs[c], k)

    def half_trip(src, dst, j, j_prev, carries):
        put_scores(dst, j - 1)
        _, v_prev = kv(j_prev)
        for c in chains:
            acc_ref[c] += _dot(a_ref[src, c], v_prev)
        z = [z_ref[src, c] for c in chains]
        sp = [jnp.maximum(z[c], 0.0) + jnp.log(1.0 + jnp.exp(-jnp.abs(z[c]))) for c in chains]
        parts = [_split2(sp[c]) for c in chains]
        incl = [_dot(parts[c][0], suffix) + _dot(parts[c][1], suffix) for c in chains]
        for c in chains:
            a_ref[dst, c] = jnp.exp(z[c] - incl[c] - carries[c]).astype(BF16)
        return [carries[c] + jnp.sum(sp[c], axis=-1, keepdims=True) for c in chains]

    def body(pair, loop_state):
        j_prev, carries = loop_state
        j = n_main - 1 - 2 * pair
        carries = half_trip(0, 1, j, j_prev, carries)
        carries = half_trip(1, 0, j - 1, j, carries)
        return j - 1, carries

    put_scores(0, n_main - 1)
    for c in chains:
        a_ref[0, c] = jnp.zeros((rc, tk), BF16)
        acc_ref[c] = states[c][1]
    j_last, _ = lax.fori_loop(0, n_main // 2, body, (0, [states[c][0] for c in chains]))
    _, v_last = kv(j_last)
    for c in chains:
        o_ref[c * rc:(c + 1) * rc, :] = _head_norm(acc_ref[c] + _dot(a_ref[0, c], v_last), g_ref[0])


def _fox_prompt_kernel(q_ref, k_ref, v_ref, cq_ref, ck_ref, g_ref, o_ref, z_ref, p_ref, acc_ref, *, tq, tk, n_chains):
    qi = pl.program_id(1)
    rc = tq // n_chains
    ratio = tq // tk
    qs = [q_ref[0, 0, c * rc:(c + 1) * rc, :] for c in range(n_chains)]
    cqs = [cq_ref[0, c * rc:(c + 1) * rc, :] for c in range(n_chains)]

    def kv(j):
        start = pl.multiple_of(j * tk, tk)
        return k_ref[0, 0, pl.ds(start, tk), :], v_ref[0, 0, pl.ds(start, tk), :]

    states = [(jnp.full((rc, 1), -jnp.inf, F32), jnp.zeros((rc, 1), F32), jnp.zeros((rc, HEAD_DIM), F32))
              for _ in range(n_chains)]
    for jj in range(ratio):
        k, v = kv(qi * ratio + jj)
        ck = ck_ref[0, qi * ratio + jj]
        states = _fox_step(qs, k, v, [cq - ck for cq in cqs], states, _diag_masks(jj, tk, rc, n_chains, False))

    n_main = qi * ratio
    chains = range(n_chains)

    def put_scores(slot, j):
        k, _ = kv(jnp.maximum(j, 0))
        for c in chains:
            z_ref[slot, c] = _dot_nt(qs[c], k)

    def half_trip(src, dst, j, j_prev, stats):
        put_scores(dst, j - 1)
        _, v_prev = kv(j_prev)
        ck = ck_ref[0, j]
        out = []
        for c in chains:
            m, l = stats[c]
            s = z_ref[src, c] + (cqs[c] - ck)
            m_new = jnp.maximum(m, jnp.max(s, axis=-1, keepdims=True))
            alpha = jnp.exp(m - m_new)
            p = jnp.exp(s - m_new)
            acc_ref[c] = (acc_ref[c] + _dot(p_ref[src, c], v_prev)) * alpha
            p_ref[dst, c] = p.astype(BF16)
            out.append((m_new, alpha * l + jnp.sum(p, axis=-1, keepdims=True)))
        return out

    def body(pair, loop_state):
        j_prev, stats = loop_state
        j = n_main - 1 - 2 * pair
        stats = half_trip(0, 1, j, j_prev, stats)
        stats = half_trip(1, 0, j - 1, j, stats)
        return j - 1, stats

    put_scores(0, n_main - 1)
    for c in chains:
        p_ref[0, c] = jnp.zeros((rc, tk), BF16)
        acc_ref[c] = states[c][2]
    j_last, stats = lax.fori_loop(0, n_main // 2, body, (0, [(states[c][0], states[c][1]) for c in chains]))
    _, v_last = kv(j_last)
    for c in chains:
        acc = acc_ref[c] + _dot(p_ref[0, c], v_last)
        o_ref[c * rc:(c + 1) * rc, :] = _head_norm(acc / stats[c][1], g_ref[0])


ATTN_TQ, ATTN_TK, ATTN_CHAINS = 512, 256, 2


def _prompt_attention(qkv, c_rows, g_sb, g_fx):
    t = qkv.shape[2]
    tq, tk = min(ATTN_TQ, t), min(ATTN_TK, t)
    cfg = dict(tq=tq, tk=tk, n_chains=ATTN_CHAINS)
    rc = tq // ATTN_CHAINS
    assert (tq // tk) % 2 == 0, "the pipelined sweep walks key blocks in pairs"
    scratch = [pltpu.VMEM((2, ATTN_CHAINS, rc, tk), F32), pltpu.VMEM((2, ATTN_CHAINS, rc, tk), BF16),
               pltpu.VMEM((ATTN_CHAINS, rc, HEAD_DIM), F32)]
    q_spec = lambda n: pl.BlockSpec((1, 1, tq, HEAD_DIM), lambda h, i: (n, h, i, 0))
    kv_spec = lambda n: pl.BlockSpec((1, 1, t, HEAD_DIM), lambda h, i: (n, h, 0, 0))
    g_spec = pl.BlockSpec((1, 1, HEAD_DIM), lambda h, i: (h, 0, 0))
    o_spec = pl.BlockSpec((tq, HEAD_DIM), lambda h, i: (i, h))
    o_shape = jax.ShapeDtypeStruct((t, D_GROUP), BF16)
    o_sb = pl.pallas_call(
        functools.partial(_sb_prompt_kernel, **cfg),
        grid=(N_HEADS, t // tq),
        in_specs=[q_spec(0), kv_spec(1), kv_spec(2), g_spec],
        out_specs=o_spec, out_shape=o_shape, scratch_shapes=scratch,
        compiler_params=_params(("parallel", "arbitrary")),
        name="sb_prompt",
    )(qkv, qkv, qkv, g_sb)
    cq = c_rows.reshape(N_HEADS, t, 1)
    ck = c_rows.reshape(N_HEADS, t // tk, 1, tk)
    o_fx = pl.pallas_call(
        functools.partial(_fox_prompt_kernel, **cfg),
        grid=(N_HEADS, t // tq),
        in_specs=[q_spec(3), kv_spec(4), kv_spec(5),
                  pl.BlockSpec((1, tq, 1), lambda h, i: (h, i, 0)),
                  pl.BlockSpec((1, t // tk, 1, tk), lambda h, i: (h, 0, 0, 0)),
                  g_spec],
        out_specs=o_spec, out_shape=o_shape, scratch_shapes=scratch,
        compiler_params=_params(("parallel", "arbitrary")),
        name="fox_prompt",
    )(qkv, qkv, qkv, cq, ck, g_fx)
    return o_sb, o_fx


def _sb_sample_kernel(q_ref, k_ref, v_ref, pk_ref, pv_ref, g_ref, o_ref, *, blk):
    tq = q_ref.shape[3]
    past = pk_ref.shape[3]
    qs = [q_ref[0, 0, 0]]
    states = [(jnp.zeros((tq, 1), F32), jnp.zeros((tq, HEAD_DIM), F32))]
    states = _sb_step(qs, k_ref[0, 0, 0], v_ref[0, 0, 0], _suffix_ones(tq), states,
                      _diag_masks(0, tq, tq, 1, True))
    suffix = _suffix_ones(blk)
    n_blk = past // blk

    def body(step, states):
        start = pl.multiple_of((n_blk - 1 - step) * blk, blk)
        k = pk_ref[0, 0, 0, pl.ds(start, blk), :].astype(BF16)
        v = pv_ref[0, 0, 0, pl.ds(start, blk), :].astype(BF16)
        return _sb_step(qs, k, v, suffix, states, [None])

    states = lax.fori_loop(0, n_blk, body, states)
    o_ref[...] = _head_norm(states[0][1], g_ref[0])


def _fox_sample_kernel(q_ref, k_ref, v_ref, pk_ref, pv_ref, cq_ref, cn_ref, cp_ref, g_ref, o_ref, *, blk):
    tq = q_ref.shape[3]
    past = pk_ref.shape[3]
    qs = [q_ref[0, 0, 0]]
    cq = cq_ref[0, 0]
    n_blk = past // blk

    def body(j, states):
        start = pl.multiple_of(j * blk, blk)
        k = pk_ref[0, 0, 0, pl.ds(start, blk), :].astype(BF16)
        v = pv_ref[0, 0, 0, pl.ds(start, blk), :].astype(BF16)
        return _fox_step(qs, k, v, [cq - cp_ref[0, 0, j]], states, [None])

    init = [(jnp.full((tq, 1), -jnp.inf, F32), jnp.zeros((tq, 1), F32), jnp.zeros((tq, HEAD_DIM), F32))]
    states = lax.fori_loop(0, n_blk, body, init)
    states = _fox_step(qs, k_ref[0, 0, 0], v_ref[0, 0, 0], [cq - cn_ref[0, 0]], states,
                       _diag_masks(0, tq, tq, 1, False))
    _, l, acc = states[0]
    o_ref[...] = _head_norm(acc / l, g_ref[0])


def _sample_attention(qkv, past_sb_k, past_sb_v, past_fx_k, past_fx_v, c_all, g_sb, g_fx, blk):
    b, _, past, _ = past_sb_k.shape
    tq = qkv.shape[2] // b
    qkv = qkv.reshape(6, N_HEADS, b, tq, HEAD_DIM)
    new_spec = lambda n: pl.BlockSpec((1, 1, 1, tq, HEAD_DIM), lambda bi, h: (n, h, bi, 0, 0))
    past_spec = pl.BlockSpec((1, 1, 1, past, HEAD_DIM), lambda bi, h: (0, bi, h, 0, 0))
    g_spec = pl.BlockSpec((1, 1, HEAD_DIM), lambda bi, h: (h, 0, 0))
    o_spec = pl.BlockSpec((tq, HEAD_DIM), lambda bi, h: (bi, h))
    o_shape = jax.ShapeDtypeStruct((b * tq, D_GROUP), BF16)
    o_sb = pl.pallas_call(
        functools.partial(_sb_sample_kernel, blk=blk),
        grid=(b, N_HEADS),
        in_specs=[new_spec(0), new_spec(1), new_spec(2), past_spec, past_spec, g_spec],
        out_specs=o_spec, out_shape=o_shape,
        compiler_params=_params(("parallel", "parallel")),
        name="sb_sample",
    )(qkv, qkv, qkv, past_sb_k[None], past_sb_v[None], g_sb)
    n_blk = past // blk
    cq = c_all[:, :, past:].reshape(b, N_HEADS, tq, 1)
    cn = c_all[:, :, past:].reshape(b, N_HEADS, 1, tq)
    cp = c_all[:, :, :past].reshape(b, N_HEADS, n_blk, 1, blk)
    o_fx = pl.pallas_call(
        functools.partial(_fox_sample_kernel, blk=blk),
        grid=(b, N_HEADS),
        in_specs=[new_spec(3), new_spec(4), new_spec(5), past_spec, past_spec,
                  pl.BlockSpec((1, 1, tq, 1), lambda bi, h: (bi, h, 0, 0)),
                  pl.BlockSpec((1, 1, 1, tq), lambda bi, h: (bi, h, 0, 0)),
                  pl.BlockSpec((1, 1, n_blk, 1, blk), lambda bi, h: (bi, h, 0, 0, 0)),
                  g_spec],
        out_specs=o_spec, out_shape=o_shape,
        compiler_params=_params(("parallel", "parallel")),
        name="fox_sample",
    )(qkv, qkv, qkv, past_fx_k[None], past_fx_v[None], cq, cn, cp, g_fx)
    return o_sb, o_fx


def _out_proj_kernel(x_ref, osb_ref, ofx_ref, wsb_ref, wfx_ref, o_ref):
    o_ref[...] = x_ref[...] + _dot(osb_ref[...], wsb_ref[...]) + _dot(ofx_ref[...], wfx_ref[...])


def _out_proj(x, o_sb, o_fx, w_sb, w_fx, tm):
    m, d = x.shape
    row = lambda i: (i, 0)
    full = lambda i: (0, 0)
    return pl.pallas_call(
        _out_proj_kernel,
        grid=(m // tm,),
        in_specs=[pl.BlockSpec((tm, d), row), pl.BlockSpec((tm, D_GROUP), row), pl.BlockSpec((tm, D_GROUP), row),
                  pl.BlockSpec((D_GROUP, d), full), pl.BlockSpec((D_GROUP, d), full)],
        out_specs=pl.BlockSpec((tm, d), row),
        out_shape=jax.ShapeDtypeStruct((m, d), F32),
        compiler_params=_params(("parallel",)),
        name="out_proj",
    )(x, o_sb, o_fx, w_sb, w_fx)


def _ple_kernel(x_ref, xn_ref, p_ref, wg_ref, wp_ref, gf_ref, o_ref):
    gate = jax.nn.sigmoid(_dot(xn_ref[...], wg_ref[...]))
    emb = _dot(p_ref[...].astype(BF16), wp_ref[...])
    o_ref[...] = _rms(x_ref[...] + gate * emb, gf_ref[...])


def _ple_final(x, xn, p, w_gate, w_proj, g_final, tm):
    m, d = x.shape
    ple = p.shape[1]
    row = lambda i: (i, 0)
    full = lambda i: (0, 0)
    return pl.pallas_call(
        _ple_kernel,
        grid=(m // tm,),
        in_specs=[pl.BlockSpec((tm, d), row), pl.BlockSpec((tm, d), row), pl.BlockSpec((tm, ple), row),
                  pl.BlockSpec((d, d), full), pl.BlockSpec((ple, d), full), pl.BlockSpec((1, d), full)],
        out_specs=pl.BlockSpec((tm, d), row),
        out_shape=jax.ShapeDtypeStruct((m, d), F32),
        compiler_params=_params(("parallel",)),
        name="ple_final",
    )(x, xn, p, w_gate, w_proj, g_final)


def _pad_lanes(x, n):
    return jnp.pad(x, ((0, 0), (0, n - x.shape[1])))


def _run_group(x, p, past, w, attn_blk):
    b, t, d = x.shape
    m = b * t
    tm = min(512, m)
    row = lambda g: g.reshape(1, -1)
    x0 = x.reshape(m, d)
    x1, xn = _ffn(x0, row(w["g_ffn1"]), w["ffn1_gate"], w["ffn1_up"], w["ffn1_down"], row(w["g_mix"]), tm, 512)
    qkv, k_sb, v_sb, k_fx, v_fx, lf_cols = _proj(xn, w["in_qkv"], w["in_f"], w["b_f"], tm)
    lf = lf_cols[:, :N_HEADS].reshape(b, t, N_HEADS).transpose(0, 2, 1)
    g_sb = w["g_out_sb"].reshape(N_HEADS, 1, HEAD_DIM)
    g_fx = w["g_out_fox"].reshape(N_HEADS, 1, HEAD_DIM)
    if past is None:
        c_rows = _cumsum_rows(lf.reshape(N_HEADS, t))
        o_sb, o_fx = _prompt_attention(qkv, c_rows, g_sb, g_fx)
    else:
        lf_all = jnp.concatenate([past[4], lf], axis=-1).reshape(b * N_HEADS, -1)
        total = lf_all.shape[1]
        padded = -(-total // LANES) * LANES
        c_all = _cumsum_rows(_pad_lanes(lf_all, padded))[:, :total].reshape(b, N_HEADS, total)
        o_sb, o_fx = _sample_attention(qkv, past[0], past[1], past[2], past[3], c_all, g_sb, g_fx, attn_blk)
    x2 = _out_proj(x1, o_sb, o_fx, w["out_sb"], w["out_fx"], tm)
    x3, xn3 = _ffn(x2, row(w["g_ffn2"]), w["ffn2_gate"], w["ffn2_up"], w["ffn2_down"], row(w["g_ple"]), tm, 512)
    y = _ple_final(x3, xn3, p.reshape(m, -1), w["ple_gate"], w["ple_proj"], row(w["g_final"]), min(256, m))
    heads = lambda a: a.reshape(N_HEADS, b, t, HEAD_DIM).transpose(1, 0, 2, 3)[None]
    return y.reshape(b, t, d), heads(k_sb), heads(v_sb), heads(k_fx), heads(v_fx), lf[None]


def kernel(x_prompt, x_sample, cache_sb_k, cache_sb_v, cache_fox_k, cache_fox_v, cache_fox_logf, p_prompt, p_sample, g_ffn1, w_ffn1_gate, w_ffn1_up, w_ffn1_down, g_mix, w_in, b_forget, g_out_sb, g_out_fox, w_out, g_ffn2, w_ffn2_gate, w_ffn2_up, w_ffn2_down, g_ple, w_ple_gate, w_ple_proj, g_final):
    assert g_ffn1.shape[0] == 1, "single-layer trunk"
    bf = lambda a: a.astype(BF16)
    n_qkv = 6 * D_GROUP
    w = {
        "g_ffn1": g_ffn1[0], "ffn1_gate": bf(w_ffn1_gate[0]), "ffn1_up": bf(w_ffn1_up[0]), "ffn1_down": bf(w_ffn1_down[0]),
        "g_mix": g_mix[0], "in_qkv": bf(w_in[0][:, :n_qkv]),
        "in_f": _pad_lanes(bf(w_in[0][:, n_qkv:]), LANES), "b_f": _pad_lanes(b_forget[0][None, :], LANES),
        "g_out_sb": g_out_sb[0], "g_out_fox": g_out_fox[0],
        "out_sb": bf(w_out[0][:D_GROUP]), "out_fx": bf(w_out[0][D_GROUP:]),
        "g_ffn2": g_ffn2[0], "ffn2_gate": bf(w_ffn2_gate[0]), "ffn2_up": bf(w_ffn2_up[0]), "ffn2_down": bf(w_ffn2_down[0]),
        "g_ple": g_ple[0], "ple_gate": bf(w_ple_gate[0]), "ple_proj": bf(w_ple_proj[0]), "g_final": g_final,
    }
    past = (cache_sb_k[0], cache_sb_v[0], cache_fox_k[0], cache_fox_v[0], cache_fox_logf[0])
    y_p, sb_k_p, sb_v_p, fx_k_p, fx_v_p, lf_p = _run_group(x_prompt, p_prompt[0], None, w, 256)
    y_s, sb_k_s, sb_v_s, fx_k_s, fx_v_s, lf_s = _run_group(x_sample, p_sample[0], past, w, 256)
    return (y_p, y_s, sb_k_p, sb_v_p, fx_k_p, fx_v_p, lf_p, sb_k_s, sb_v_s, fx_k_s, fx_v_s, lf_s)
```

```python
import functools
import math

import jax
import jax.numpy as jnp
from jax import lax
from jax.experimental import pallas as pl
from jax.experimental.pallas import tpu as pltpu

HEAD_DIM = 128
N_HEADS = 8
D_GROUP = N_HEADS * HEAD_DIM
MACARON = 0.5
EPS = 1e-6
SCALE = 1.0 / math.sqrt(HEAD_DIM)
LOG2E = math.log2(math.e)
LANES = 128
V7X_VMEM_LIMIT = 56 * 1024 * 1024

F32 = jnp.float32
BF16 = jnp.bfloat16


def _params(semantics):
    return pltpu.CompilerParams(dimension_semantics=semantics, vmem_limit_bytes=V7X_VMEM_LIMIT)


def _rms(x, g):
    return x * lax.rsqrt(jnp.mean(x * x, axis=-1, keepdims=True) + EPS) * g


def _dot(a, b):
    return jnp.dot(a, b, preferred_element_type=F32)


def _dot_nt(a, b):
    return lax.dot_general(a, b, (((1,), (1,)), ((), ())), preferred_element_type=F32)


def _split3(x):
    hi = x.astype(BF16)
    r = x - hi.astype(F32)
    mid = r.astype(BF16)
    lo = (r - mid.astype(F32)).astype(BF16)
    return hi, mid, lo


def _softplus2(z2):
    neg_abs = lax.bitcast_convert_type(lax.bitcast_convert_type(z2, jnp.uint32) | jnp.uint32(0x80000000), F32)
    return jnp.maximum(z2, 0.0) + jnp.log(1.0 + jnp.exp2(neg_abs)) * LOG2E


def _ffn_kernel(x_ref, g_ref, wg_ref, wu_ref, wd_ref, gn_ref, o_ref, on_ref, xn_ref):
    f = pl.program_id(1)

    @pl.when(f == 0)
    def _():
        xn_ref[...] = _rms(x_ref[...], g_ref[...]).astype(BF16)
        o_ref[...] = jnp.zeros_like(o_ref)

    xn = xn_ref[...]
    gate = _dot(xn, wg_ref[...])
    up = _dot(xn, wu_ref[...])
    h = (gate * jax.nn.sigmoid(gate) * up).astype(BF16)
    o_ref[...] += _dot(h, wd_ref[...])

    @pl.when(f == pl.num_programs(1) - 1)
    def _():
        y = x_ref[...] + MACARON * o_ref[...]
        o_ref[...] = y
        on_ref[...] = _rms(y, gn_ref[...]).astype(BF16)


def _ffn(x, g, wg, wu, wd, g_next, tm, tf):
    m, d = x.shape
    d_ff = wg.shape[1]
    row = lambda i, f: (i, 0)
    return pl.pallas_call(
        _ffn_kernel,
        grid=(m // tm, d_ff // tf),
        in_specs=[
            pl.BlockSpec((tm, d), row),
            pl.BlockSpec((1, d), lambda i, f: (0, 0)),
            pl.BlockSpec((d, tf), lambda i, f: (0, f)),
            pl.BlockSpec((d, tf), lambda i, f: (0, f)),
            pl.BlockSpec((tf, d), lambda i, f: (f, 0)),
            pl.BlockSpec((1, d), lambda i, f: (0, 0)),
        ],
        out_specs=[pl.BlockSpec((tm, d), row), pl.BlockSpec((tm, d), row)],
        out_shape=[jax.ShapeDtypeStruct((m, d), F32), jax.ShapeDtypeStruct((m, d), BF16)],
        scratch_shapes=[pltpu.VMEM((tm, d), BF16)],
        compiler_params=_params(("parallel", "arbitrary")),
        name="ffn",
    )(x, g, wg, wu, wd, g_next)


def _proj_kernel(xn_ref, w_ref, wf_ref, bf_ref, qkv_ref, ksb_ref, vsb_ref, kfx_ref, vfx_ref, lf_ref):
    n = pl.program_id(1)
    xn = xn_ref[...]
    res = _dot(xn, w_ref[...])
    is_q = jnp.logical_or(n == 0, n == 3)
    res_mx = res * jnp.where(is_q, SCALE * LOG2E, 1.0)
    for h in range(N_HEADS):
        qkv_ref[0, h] = res_mx[:, h * HEAD_DIM:(h + 1) * HEAD_DIM].astype(BF16)

    def store_heads(ref):
        for h in range(N_HEADS):
            ref[h] = res[:, h * HEAD_DIM:(h + 1) * HEAD_DIM]

    for idx, ref in ((1, ksb_ref), (2, vsb_ref), (4, kfx_ref), (5, vfx_ref)):
        pl.when(n == idx)(functools.partial(store_heads, ref))

    @pl.when(n == 0)
    def _():
        fz = _dot(xn, wf_ref[...]) + bf_ref[...]
        lf_ref[...] = jnp.minimum(fz, 0.0) - jnp.log1p(jnp.exp(-jnp.abs(fz)))


def _proj(xn, w_qkv, w_f, b_f, tm):
    m, d = xn.shape
    head_f32 = jax.ShapeDtypeStruct((N_HEADS, m, HEAD_DIM), F32)
    head_spec = pl.BlockSpec((N_HEADS, tm, HEAD_DIM), lambda i, n: (0, i, 0))
    return pl.pallas_call(
        _proj_kernel,
        grid=(m // tm, 6),
        in_specs=[
            pl.BlockSpec((tm, d), lambda i, n: (i, 0)),
            pl.BlockSpec((d, D_GROUP), lambda i, n: (0, n)),
            pl.BlockSpec((d, LANES), lambda i, n: (0, 0)),
            pl.BlockSpec((1, LANES), lambda i, n: (0, 0)),
        ],
        out_specs=[
            pl.BlockSpec((1, N_HEADS, tm, HEAD_DIM), lambda i, n: (n, 0, i, 0)),
            head_spec, head_spec, head_spec, head_spec,
            pl.BlockSpec((tm, LANES), lambda i, n: (i, 0)),
        ],
        out_shape=[
            jax.ShapeDtypeStruct((6, N_HEADS, m, HEAD_DIM), BF16),
            head_f32, head_f32, head_f32, head_f32,
            jax.ShapeDtypeStruct((m, LANES), F32),
        ],
        compiler_params=_params(("parallel", "arbitrary")),
        name="in_proj",
    )(xn, w_qkv, w_f, b_f)


def _cumsum_kernel(x_ref, o_ref):
    n_chunks, rows, _ = x_ref.shape
    upper = (lax.broadcasted_iota(jnp.int32, (LANES, LANES), 0)
             <= lax.broadcasted_iota(jnp.int32, (LANES, LANES), 1)).astype(BF16)

    def body(c, carry):
        hi, mid, lo = _split3(x_ref[c])
        cs = _dot(hi, upper) + _dot(mid, upper) + _dot(lo, upper) + carry
        o_ref[c] = cs * LOG2E
        return cs[:, LANES - 1:LANES]

    lax.fori_loop(0, n_chunks, body, jnp.zeros((rows, 1), F32))


def _cumsum_rows(x):
    rows, length = x.shape
    xc = x.reshape(rows, length // LANES, LANES).transpose(1, 0, 2)
    out = pl.pallas_call(
        _cumsum_kernel,
        out_shape=jax.ShapeDtypeStruct(xc.shape, F32),
        compiler_params=pltpu.CompilerParams(vmem_limit_bytes=V7X_VMEM_LIMIT),
        name="cumsum",
    )(xc)
    return out.transpose(1, 0, 2).reshape(rows, length)


def _suffix_ones(n):
    return (lax.broadcasted_iota(jnp.int32, (n, n), 0)
            >= lax.broadcasted_iota(jnp.int32, (n, n), 1)).astype(BF16)


def _sb_step(qs, k, v, suffix, states, masks):
    live = [c for c, m in enumerate(masks) if not isinstance(m, str)]
    z = {c: _dot_nt(qs[c], k) for c in live}
    sp = {}
    for c in live:
        s = _softplus2(z[c])
        sp[c] = s if masks[c] is None else jnp.where(masks[c], s, 0.0)
    incl = {c: _dot(sp[c].astype(BF16), suffix) for c in live}
    out = list(states)
    for c in live:
        carry, acc = states[c]
        a = jnp.exp2(z[c] - incl[c] - carry)
        if masks[c] is not None:
            a = jnp.where(masks[c], a, 0.0)
        out[c] = (carry + incl[c][:, :1], acc + _dot(a.astype(BF16), v))
    return out


def _fox_step(qs, k, v, biases, states, masks):
    live = [c for c, m in enumerate(masks) if not isinstance(m, str)]
    s = {}
    for c in live:
        sc = _dot_nt(qs[c], k) + biases[c]
        s[c] = sc if masks[c] is None else jnp.where(masks[c], sc, -jnp.inf)
    out = list(states)
    for c in live:
        m, l, acc = states[c]
        m_new = jnp.maximum(m, jnp.max(s[c], axis=-1, keepdims=True))
        alpha = jnp.exp2(m - m_new)
        p = jnp.exp2(s[c] - m_new)
        out[c] = (m_new, alpha * l + jnp.sum(p, axis=-1, keepdims=True), alpha * acc + _dot(p.astype(BF16), v))
    return out


def _diag_masks(jj, tk, rc, n_chains, strict):
    masks = []
    for c in range(n_chains):
        lo_row, hi_row = c * rc, (c + 1) * rc - 1
        lo_key, hi_key = jj * tk, (jj + 1) * tk - 1
        if (lo_key >= hi_row) if strict else (lo_key > hi_row):
            masks.append("skip")
        elif (hi_key < lo_row) if strict else (hi_key <= lo_row):
            masks.append(None)
        else:
            rows = lax.broadcasted_iota(jnp.int32, (rc, tk), 0) + lo_row
            cols = lax.broadcasted_iota(jnp.int32, (rc, tk), 1) + lo_key
            masks.append(cols < rows if strict else cols <= rows)
    return masks


def _head_norm(o, g):
    return (o * lax.rsqrt(jnp.mean(o * o, axis=-1, keepdims=True) + EPS) * g).astype(BF16)


def _sb_prompt_kernel(q_ref, k_ref, v_ref, g_ref, o_ref, z_ref, a_ref, acc_ref, *, tq, tk, n_chains):
    qi = pl.program_id(1)
    rc = tq // n_chains
    ratio = tq // tk
    qs = [q_ref[0, 0, c * rc:(c + 1) * rc, :] for c in range(n_chains)]
    suffix = _suffix_ones(tk)

    def kv(j):
        start = pl.multiple_of(j * tk, tk)
        return k_ref[0, 0, pl.ds(start, tk), :], v_ref[0, 0, pl.ds(start, tk), :]

    states = [(jnp.zeros((rc, 1), F32), jnp.zeros((rc, HEAD_DIM), F32)) for _ in range(n_chains)]
    for jj in reversed(range(ratio)):
        k, v = kv(qi * ratio + jj)
        states = _sb_step(qs, k, v, suffix, states, _diag_masks(jj, tk, rc, n_chains, True))

    n_main = qi * ratio
    chains = range(n_chains)

    def put_scores(slot, j):
        k, _ = kv(jnp.maximum(j, 0))
        for c in chains:
            z_ref[slot, c] = _dot_nt(qs[c], k)

    def half_trip(src, dst, j, j_prev, carries):
        put_scores(dst, j - 1)
        _, v_prev = kv(j_prev)
        for c in chains:
            acc_ref[c] += _dot(a_ref[src, c], v_prev)
        z = [z_ref[src, c] for c in chains]
        incl = [_dot(_softplus2(z[c]).astype(BF16), suffix) for c in chains]
        for c in chains:
            a_ref[dst, c] = jnp.exp2(z[c] - incl[c] - carries[c]).astype(BF16)
        return [carries[c] + incl[c][:, :1] for c in chains]

    def body(pair, loop_state):
        j_prev, carries = loop_state
        j = n_main - 1 - 2 * pair
        carries = half_trip(0, 1, j, j_prev, carries)
        carries = half_trip(1, 0, j - 1, j, carries)
        return j - 1, carries

    put_scores(0, n_main - 1)
    for c in chains:
        a_ref[0, c] = jnp.zeros((rc, tk), BF16)
        acc_ref[c] = states[c][1]
    j_last, _ = lax.fori_loop(0, n_main // 2, body, (0, [states[c][0] for c in chains]))
    _, v_last = kv(j_last)
    for c in chains:
        o_ref[c * rc:(c + 1) * rc, :] = _head_norm(acc_ref[c] + _dot(a_ref[0, c], v_last), g_ref[0])


def _fox_prompt_kernel(q_ref, k_ref, v_ref, cq_ref, ck_ref, g_ref, o_ref, z_ref, p_ref, acc_ref, *, tq, tk, n_chains):
    qi = pl.program_id(1)
    rc = tq // n_chains
    ratio = tq // tk
    qs = [q_ref[0, 0, c * rc:(c + 1) * rc, :] for c in range(n_chains)]
    cqs = [cq_ref[0, c * rc:(c + 1) * rc, :] for c in range(n_chains)]

    def kv(j):
        start = pl.multiple_of(j * tk, tk)
        return k_ref[0, 0, pl.ds(start, tk), :], v_ref[0, 0, pl.ds(start, tk), :]

    states = [(jnp.full((rc, 1), -jnp.inf, F32), jnp.zeros((rc, 1), F32), jnp.zeros((rc, HEAD_DIM), F32))
              for _ in range(n_chains)]
    for jj in range(ratio):
        k, v = kv(qi * ratio + jj)
        ck = ck_ref[0, qi * ratio + jj]
        states = _fox_step(qs, k, v, [cq - ck for cq in cqs], states, _diag_masks(jj, tk, rc, n_chains, False))

    n_main = qi * ratio
    chains = range(n_chains)

    def put_scores(slot, j):
        k, _ = kv(jnp.maximum(j, 0))
        for c in chains:
            z_ref[slot, c] = _dot_nt(qs[c], k)

    def half_trip(src, dst, j, j_prev, stats):
        put_scores(dst, j - 1)
        _, v_prev = kv(j_prev)
        ck = ck_ref[0, j]
        out = []
        for c in chains:
            m, l = stats[c]
            s = z_ref[src, c] + (cqs[c] - ck)
            m_new = jnp.maximum(m, jnp.max(s, axis=-1, keepdims=True))
            alpha = jnp.exp2(m - m_new)
            p = jnp.exp2(s - m_new)
            acc_ref[c] =(acc_ref[c] + _dot(p_ref[src, c], v_prev)) * alpha
            p_ref[dst, c] = p.astype(BF16)
            out.append((m_new, alpha * l + jnp.sum(p, axis=-1, keepdims=True)))
        return out

    def body(pair, loop_state):
        j_prev, stats = loop_state
        j = n_main - 1 - 2 * pair
        stats = half_trip(0, 1, j, j_prev, stats)
        stats = half_trip(1, 0, j - 1, j, stats)
        return j - 1, stats

    put_scores(0, n_main - 1)
    for c in chains:
        p_ref[0, c] = jnp.zeros((rc, tk), BF16)
        acc_ref[c] = states[c][2]
    j_last, stats = lax.fori_loop(0, n_main // 2, body, (0, [(states[c][0], states[c][1]) for c in chains]))
    _, v_last = kv(j_last)
    for c in chains:
        acc = acc_ref[c] + _dot(p_ref[0, c], v_last)
        o_ref[c * rc:(c + 1) * rc, :] = _head_norm(acc / stats[c][1], g_ref[0])


ATTN_TQ, ATTN_TK, ATTN_CHAINS = 512, 256, 2


def _prompt_attention(qkv, c_rows, g_sb, g_fx):
    t = qkv.shape[2]
    tq, tk = min(ATTN_TQ, t), min(ATTN_TK, t)
    cfg = dict(tq=tq, tk=tk, n_chains=ATTN_CHAINS)
    rc = tq // ATTN_CHAINS
    assert (tq // tk) % 2 == 0, "the pipelined sweep walks key blocks in pairs"
    scratch = [pltpu.VMEM((2, ATTN_CHAINS, rc, tk), F32), pltpu.VMEM((2, ATTN_CHAINS, rc, tk), BF16),
               pltpu.VMEM((ATTN_CHAINS, rc, HEAD_DIM), F32)]
    q_spec = lambda n: pl.BlockSpec((1, 1, tq, HEAD_DIM), lambda h, i: (n, h, i, 0))
    kv_spec = lambda n: pl.BlockSpec((1, 1, t, HEAD_DIM), lambda h, i: (n, h, 0, 0))
    g_spec = pl.BlockSpec((1, 1, HEAD_DIM), lambda h, i: (h, 0, 0))
    o_spec = pl.BlockSpec((tq, HEAD_DIM), lambda h, i: (i, h))
    o_shape = jax.ShapeDtypeStruct((t, D_GROUP), BF16)
    o_sb = pl.pallas_call(
        functools.partial(_sb_prompt_kernel, **cfg),
        grid=(N_HEADS, t // tq),
        in_specs=[q_spec(0), kv_spec(1), kv_spec(2), g_spec],
        out_specs=o_spec, out_shape=o_shape, scratch_shapes=scratch,
        compiler_params=_params(("parallel", "arbitrary")),
        name="sb_prompt",
    )(qkv, qkv, qkv, g_sb)
    cq = c_rows.reshape(N_HEADS, t, 1)
    ck = c_rows.reshape(N_HEADS, t // tk, 1, tk)
    o_fx = pl.pallas_call(
        functools.partial(_fox_prompt_kernel, **cfg),
        grid=(N_HEADS, t // tq),
        in_specs=[q_spec(3), kv_spec(4), kv_spec(5),
                  pl.BlockSpec((1, tq, 1), lambda h, i: (h, i, 0)),
                  pl.BlockSpec((1, t // tk, 1, tk), lambda h, i: (h, 0, 0, 0)),
                  g_spec],
        out_specs=o_spec, out_shape=o_shape, scratch_shapes=scratch,
        compiler_params=_params(("parallel", "arbitrary")),
        name="fox_prompt",
    )(qkv, qkv, qkv, cq, ck, g_fx)
    return o_sb, o_fx


def _sb_sample_kernel(q_ref, k_ref, v_ref, pk_ref, pv_ref, g_ref, o_ref, *, blk):
    tq = q_ref.shape[3]
    past = pk_ref.shape[3]
    q = q_ref[0, 0, 0]
    states = [(jnp.zeros((tq, 1), F32), jnp.zeros((tq, HEAD_DIM), F32))]
    (carry, acc), = _sb_step([q], k_ref[0, 0, 0], v_ref[0, 0, 0], _suffix_ones(tq), states,
                             _diag_masks(0, tq, tq, 1, True))
    suffix = _suffix_ones(blk)
    blocks = range(past // blk)
    k_blk = lambda j: pk_ref[0, 0, 0, j * blk:(j + 1) * blk, :].astype(BF16)
    v_blk = lambda j: pv_ref[0, 0, 0, j * blk:(j + 1) * blk, :].astype(BF16)
    z = [_dot_nt(q, k_blk(j)) for j in blocks]
    incl = [_dot(_softplus2(z[j]).astype(BF16), suffix) for j in blocks]
    for j in reversed(blocks):
        acc = acc + _dot(jnp.exp2(z[j] - incl[j] - carry).astype(BF16), v_blk(j))
        carry = carry + incl[j][:, :1]
    o_ref[...] = _head_norm(acc, g_ref[0])


def _fox_sample_kernel(q_ref, k_ref, v_ref, pk_ref, pv_ref, cq_ref, cn_ref, cp_ref, g_ref, o_ref, *, blk):
    tq = q_ref.shape[3]
    past = pk_ref.shape[3]
    q = q_ref[0, 0, 0]
    cq = cq_ref[0, 0]
    blocks = range(past // blk)
    k_blk = lambda j: pk_ref[0, 0, 0, j * blk:(j + 1) * blk, :].astype(BF16)
    v_blk = lambda j: pv_ref[0, 0, 0, j * blk:(j + 1) * blk, :].astype(BF16)
    (mask,) = _diag_masks(0, tq, tq, 1, False)
    s_new = jnp.where(mask, _dot_nt(q, k_ref[0, 0, 0]) + (cq - cn_ref[0, 0]), -jnp.inf)
    s = [_dot_nt(q, k_blk(j)) + (cq - cp_ref[0, 0, j]) for j in blocks]
    m = jnp.max(s_new, axis=-1, keepdims=True)
    for j in blocks:
        m = jnp.maximum(m, jnp.max(s[j], axis=-1, keepdims=True))
    p_new = jnp.exp2(s_new - m)
    l = jnp.sum(p_new, axis=-1, keepdims=True)
    acc = _dot(p_new.astype(BF16), v_ref[0, 0, 0])
    for j in blocks:
        p = jnp.exp2(s[j] - m)
        l = l + jnp.sum(p, axis=-1, keepdims=True)
        acc = acc + _dot(p.astype(BF16), v_blk(j))
    o_ref[...] = _head_norm(acc / l, g_ref[0])


def _sample_attention(qkv, past_sb_k, past_sb_v, past_fx_k, past_fx_v, c_all, g_sb, g_fx, blk):
    b, _, past, _ = past_sb_k.shape
    tq = qkv.shape[2] // b
    qkv = qkv.reshape(6, N_HEADS, b, tq, HEAD_DIM)
    new_spec = lambda n: pl.BlockSpec((1, 1, 1, tq, HEAD_DIM), lambda bi, h: (n, h, bi, 0, 0))
    past_spec = pl.BlockSpec((1, 1, 1, past, HEAD_DIM), lambda bi, h: (0, bi, h, 0, 0))
    g_spec = pl.BlockSpec((1, 1, HEAD_DIM), lambda bi, h: (h, 0, 0))
    o_spec = pl.BlockSpec((tq, HEAD_DIM), lambda bi, h: (bi, h))
    o_shape = jax.ShapeDtypeStruct((b * tq, D_GROUP), BF16)
    o_sb = pl.pallas_call(
        functools.partial(_sb_sample_kernel, blk=blk),
        grid=(b, N_HEADS),
        in_specs=[new_spec(0), new_spec(1), new_spec(2), past_spec, past_spec, g_spec],
        out_specs=o_spec, out_shape=o_shape,
        compiler_params=_params(("parallel", "parallel")),
        name="sb_sample",
    )(qkv, qkv, qkv, past_sb_k[None], past_sb_v[None], g_sb)
    n_blk = past // blk
    cq = c_all[:, :, past:].reshape(b, N_HEADS, tq, 1)
    cn = c_all[:, :, past:].reshape(b, N_HEADS, 1, tq)
    cp = c_all[:, :, :past].reshape(b, N_HEADS, n_blk, 1, blk)
    o_fx = pl.pallas_call(
        functools.partial(_fox_sample_kernel, blk=blk),
        grid=(b, N_HEADS),
        in_specs=[new_spec(3), new_spec(4), new_spec(5), past_spec, past_spec,
                  pl.BlockSpec((1, 1, tq, 1), lambda bi, h: (bi, h, 0, 0)),
                  pl.BlockSpec((1, 1, 1, tq), lambda bi, h: (bi, h, 0, 0)),
                  pl.BlockSpec((1, 1, n_blk, 1, blk), lambda bi, h: (bi, h, 0, 0, 0)),
                  g_spec],
        out_specs=o_spec, out_shape=o_shape,
        compiler_params=_params(("parallel", "parallel")),
        name="fox_sample",
    )(qkv, qkv, qkv, past_fx_k[None], past_fx_v[None], cq, cn, cp, g_fx)
    return o_sb, o_fx


def _out_proj_kernel(x_ref, osb_ref, ofx_ref, wsb_ref, wfx_ref, o_ref):
    o_ref[...] = x_ref[...] + _dot(osb_ref[...], wsb_ref[...]) + _dot(ofx_ref[...], wfx_ref[...])


def _out_proj(x, o_sb, o_fx, w_sb, w_fx, tm):
    m, d = x.shape
    row = lambda i: (i, 0)
    full = lambda i: (0, 0)
    return pl.pallas_call(
        _out_proj_kernel,
        grid=(m // tm,),
        in_specs=[pl.BlockSpec((tm, d), row), pl.BlockSpec((tm, D_GROUP), row), pl.BlockSpec((tm, D_GROUP), row),
                  pl.BlockSpec((D_GROUP, d), full), pl.BlockSpec((D_GROUP, d), full)],
        out_specs=pl.BlockSpec((tm, d), row),
        out_shape=jax.ShapeDtypeStruct((m, d), F32),
        compiler_params=_params(("parallel",)),
        name="out_proj",
    )(x, o_sb, o_fx, w_sb, w_fx)


def _ple_kernel(x_ref, xn_ref, p_ref, wg_ref, wp_ref, gf_ref, o_ref):
    gate = jax.nn.sigmoid(_dot(xn_ref[...], wg_ref[...]))
    emb = _dot(p_ref[...].astype(BF16), wp_ref[...])
    o_ref[...] = _rms(x_ref[...] + gate * emb, gf_ref[...])


def _ple_final(x, xn, p, w_gate, w_proj, g_final, tm):
    m, d = x.shape
    ple = p.shape[1]
    row = lambda i: (i, 0)
    full = lambda i: (0, 0)
    return pl.pallas_call(
        _ple_kernel,
        grid=(m // tm,),
        in_specs=[pl.BlockSpec((tm, d), row), pl.BlockSpec((tm, d), row), pl.BlockSpec((tm, ple), row),
                  pl.BlockSpec((d, d), full), pl.BlockSpec((ple, d), full), pl.BlockSpec((1, d), full)],
        out_specs=pl.BlockSpec((tm, d), row),
        out_shape=jax.ShapeDtypeStruct((m, d), F32),
        compiler_params=_params(("parallel",)),
        name="ple_final",
    )(x, xn, p, w_gate, w_proj, g_final)


def _pad_lanes(x, n):
    return jnp.pad(x, ((0, 0), (0, n - x.shape[1])))


def _run_group(x, p, past, w, attn_blk):
    b, t, d = x.shape
    m = b * t
    tm = min(512, m)
    row = lambda g: g.reshape(1, -1)
    x0 = x.reshape(m, d)
    x1, xn = _ffn(x0, row(w["g_ffn1"]), w["ffn1_gate"], w["ffn1_up"], w["ffn1_down"], row(w["g_mix"]), tm, 512)
    qkv, k_sb, v_sb, k_fx, v_fx, lf_cols = _proj(xn, w["in_qkv"], w["in_f"], w["b_f"], tm)
    lf = lf_cols[:, :N_HEADS].reshape(b, t, N_HEADS).transpose(0, 2, 1)
    g_sb = w["g_out_sb"].reshape(N_HEADS, 1, HEAD_DIM)
    g_fx = w["g_out_fox"].reshape(N_HEADS, 1, HEAD_DIM)
    if past is None:
        c_rows = _cumsum_rows(lf.reshape(N_HEADS, t))
        o_sb, o_fx = _prompt_attention(qkv, c_rows, g_sb, g_fx)
    else:
        lf_all = jnp.concatenate([past[4], lf], axis=-1).reshape(b * N_HEADS, -1)
        total = lf_all.shape[1]
        padded = -(-total // LANES) * LANES
        c_all = _cumsum_rows(_pad_lanes(lf_all, padded))[:, :total].reshape(b, N_HEADS, total)
        o_sb, o_fx = _sample_attention(qkv, past[0], past[1], past[2], past[3], c_all, g_sb, g_fx, attn_blk)
    x2 = _out_proj(x1, o_sb, o_fx, w["out_sb"], w["out_fx"], tm)
    x3, xn3 = _ffn(x2, row(w["g_ffn2"]), w["ffn2_gate"], w["ffn2_up"], w["ffn2_down"], row(w["g_ple"]), tm, 512)
    y = _ple_final(x3, xn3, p.reshape(m, -1), w["ple_gate"], w["ple_proj"], row(w["g_final"]), min(256, m))
    heads = lambda a: a.reshape(N_HEADS, b, t, HEAD_DIM).transpose(1, 0, 2, 3)[None]
    return y.reshape(b, t, d), heads(k_sb), heads(v_sb), heads(k_fx), heads(v_fx), lf[None]


def kernel(x_prompt, x_sample, cache_sb_k, cache_sb_v, cache_fox_k, cache_fox_v, cache_fox_logf, p_prompt, p_sample, g_ffn1, w_ffn1_gate, w_ffn1_up, w_ffn1_down, g_mix, w_in, b_forget, g_out_sb, g_out_fox, w_out, g_ffn2, w_ffn2_gate, w_ffn2_up, w_ffn2_down, g_ple, w_ple_gate, w_ple_proj, g_final):
    assert g_ffn1.shape[0] == 1, "single-layer trunk"
    bf = lambda a: a.astype(BF16)
    n_qkv = 6 * D_GROUP
    w = {
        "g_ffn1": g_ffn1[0], "ffn1_gate": bf(w_ffn1_gate[0]), "ffn1_up": bf(w_ffn1_up[0]), "ffn1_down": bf(w_ffn1_down[0]),
        "g_mix": g_mix[0], "in_qkv": bf(w_in[0][:, :n_qkv]),
        "in_f": _pad_lanes(bf(w_in[0][:, n_qkv:]), LANES), "b_f": _pad_lanes(b_forget[0][None, :], LANES),
        "g_out_sb": g_out_sb[0], "g_out_fox": g_out_fox[0],
        "out_sb": bf(w_out[0][:D_GROUP]), "out_fx": bf(w_out[0][D_GROUP:]),
        "g_ffn2": g_ffn2[0], "ffn2_gate": bf(w_ffn2_gate[0]), "ffn2_up": bf(w_ffn2_up[0]), "ffn2_down": bf(w_ffn2_down[0]),
        "g_ple": g_ple[0], "ple_gate": bf(w_ple_gate[0]), "ple_proj": bf(w_ple_proj[0]), "g_final": g_final,
    }
    past = (cache_sb_k[0], cache_sb_v[0], cache_fox_k[0], cache_fox_v[0], cache_fox_logf[0])
    y_p, sb_k_p, sb_v_p, fx_k_p, fx_v_p, lf_p = _run_group(x_prompt, p_prompt[0], None, w, 256)
    y_s, sb_k_s, sb_v_s, fx_k_s, fx_v_s, lf_s = _run_group(x_sample, p_sample[0], past, w, 256)
    return (y_p, y_s, sb_k_p, sb_v_p, fx_k_p, fx_v_p, lf_p, sb_k_s, sb_v_s, fx_k_s, fx_v_s, lf_s)
```

```python
import functools
import math

import jax
import jax.numpy as jnp
from jax import lax
from jax.experimental import pallas as pl
from jax.experimental.pallas import tpu as pltpu

HEAD_DIM = 128
N_HEADS = 8
D_GROUP = N_HEADS * HEAD_DIM
MACARON = 0.5
EPS = 1e-6
SCALE = 1.0 / math.sqrt(HEAD_DIM)
LOG2E = math.log2(math.e)
LANES = 128
V7X_VMEM_LIMIT = 56 * 1024 * 1024

F32 = jnp.float32
BF16 = jnp.bfloat16


def _params(semantics):
    return pltpu.CompilerParams(dimension_semantics=semantics, vmem_limit_bytes=V7X_VMEM_LIMIT)


def _rms(x, g):
    return x * lax.rsqrt(jnp.mean(x * x, axis=-1, keepdims=True) + EPS) * g


def _dot(a, b):
    return jnp.dot(a, b, preferred_element_type=F32)


def _dot_nt(a, b):
    return lax.dot_general(a, b, (((1,), (1,)), ((), ())), preferred_element_type=F32)


def _split3(x):
    hi = x.astype(BF16)
    r = x - hi.astype(F32)
    mid = r.astype(BF16)
    lo = (r - mid.astype(F32)).astype(BF16)
    return hi, mid, lo


def _softplus2(z2):
    return jnp.maximum(z2, 0.0) + jnp.log(1.0 + jnp.exp2(-jnp.abs(z2))) * LOG2E


def _ffn_kernel(x_ref, g_ref, wg_ref, wu_ref, wd_ref, gn_ref, o_ref, on_ref, xn_ref):
    f = pl.program_id(1)

    @pl.when(f == 0)
    def _():
        xn_ref[...] = _rms(x_ref[...], g_ref[...]).astype(BF16)
        o_ref[...] = jnp.zeros_like(o_ref)

    xn = xn_ref[...]
    gate = _dot(xn, wg_ref[...])
    up = _dot(xn, wu_ref[...])
    h = (gate * jax.nn.sigmoid(gate) * up).astype(BF16)
    o_ref[...] += _dot(h, wd_ref[...])

    @pl.when(f == pl.num_programs(1) - 1)
    def _():
        y = x_ref[...] + MACARON * o_ref[...]
        o_ref[...] = y
        on_ref[...] = _rms(y, gn_ref[...]).astype(BF16)


def _ffn(x, g, wg, wu, wd, g_next, tm, tf):
    m, d = x.shape
    d_ff = wg.shape[1]
    row = lambda i, f: (i, 0)
    return pl.pallas_call(
        _ffn_kernel,
        grid=(m // tm, d_ff // tf),
        in_specs=[
            pl.BlockSpec((tm, d), row),
            pl.BlockSpec((1, d), lambda i, f: (0, 0)),
            pl.BlockSpec((d, tf), lambda i, f: (0, f)),
            pl.BlockSpec((d, tf), lambda i, f: (0, f)),
            pl.BlockSpec((tf, d), lambda i, f: (f, 0)),
            pl.BlockSpec((1, d), lambda i, f: (0, 0)),
        ],
        out_specs=[pl.BlockSpec((tm, d), row), pl.BlockSpec((tm, d), row)],
        out_shape=[jax.ShapeDtypeStruct((m, d), F32), jax.ShapeDtypeStruct((m, d), BF16)],
        scratch_shapes=[pltpu.VMEM((tm, d), BF16)],
        compiler_params=_params(("parallel", "arbitrary")),
        name="ffn",
    )(x, g, wg, wu, wd, g_next)


def _proj_kernel(xn_ref, w_ref, wf_ref, bf_ref, qkv_ref, ksb_ref, vsb_ref, kfx_ref, vfx_ref, lf_ref):
    n = pl.program_id(1)
    xn = xn_ref[...]
    res = _dot(xn, w_ref[...])
    is_q = jnp.logical_or(n == 0, n == 3)
    res_mx = res * jnp.where(is_q, SCALE * LOG2E, 1.0)
    for h in range(N_HEADS):
        qkv_ref[0, h] = res_mx[:, h * HEAD_DIM:(h + 1) * HEAD_DIM].astype(BF16)

    def store_heads(ref):
        for h in range(N_HEADS):
            ref[h] = res[:, h * HEAD_DIM:(h + 1) * HEAD_DIM]

    for idx, ref in ((1, ksb_ref), (2, vsb_ref), (4, kfx_ref), (5, vfx_ref)):
        pl.when(n == idx)(functools.partial(store_heads, ref))

    @pl.when(n == 0)
    def _():
        fz = _dot(xn, wf_ref[...]) + bf_ref[...]
        lf_ref[...] = jnp.minimum(fz, 0.0) - jnp.log1p(jnp.exp(-jnp.abs(fz)))


def _proj(xn, w_qkv, w_f, b_f, tm):
    m, d = xn.shape
    head_f32 = jax.ShapeDtypeStruct((N_HEADS, m, HEAD_DIM), F32)
    head_spec = pl.BlockSpec((N_HEADS, tm, HEAD_DIM), lambda i, n: (0, i, 0))
    return pl.pallas_call(
        _proj_kernel,
        grid=(m // tm, 6),
        in_specs=[
            pl.BlockSpec((tm, d), lambda i, n: (i, 0)),
            pl.BlockSpec((d, D_GROUP), lambda i, n: (0, n)),
            pl.BlockSpec((d, LANES), lambda i, n: (0, 0)),
            pl.BlockSpec((1, LANES), lambda i, n: (0, 0)),
        ],
        out_specs=[
            pl.BlockSpec((1, N_HEADS, tm, HEAD_DIM), lambda i, n: (n, 0, i, 0)),
            head_spec, head_spec, head_spec, head_spec,
            pl.BlockSpec((tm, LANES), lambda i, n: (i, 0)),
        ],
        out_shape=[
            jax.ShapeDtypeStruct((6, N_HEADS, m, HEAD_DIM), BF16),
            head_f32, head_f32, head_f32, head_f32,
            jax.ShapeDtypeStruct((m, LANES), F32),
        ],
        compiler_params=_params(("parallel", "arbitrary")),
        name="in_proj",
    )(xn, w_qkv, w_f, b_f)


def _cumsum_kernel(x_ref, o_ref):
    n_chunks, rows, _ = x_ref.shape
    upper = (lax.broadcasted_iota(jnp.int32, (LANES, LANES), 0)
             <= lax.broadcasted_iota(jnp.int32, (LANES, LANES), 1)).astype(BF16)

    def body(c, carry):
        hi, mid, lo = _split3(x_ref[c])
        cs = _dot(hi, upper) + _dot(mid, upper) + _dot(lo, upper) + carry
        o_ref[c] = cs * LOG2E
        return cs[:, LANES - 1:LANES]

    lax.fori_loop(0, n_chunks, body, jnp.zeros((rows, 1), F32))


def _cumsum_rows(x):
    rows, length = x.shape
    xc = x.reshape(rows, length // LANES, LANES).transpose(1, 0, 2)
    out = pl.pallas_call(
        _cumsum_kernel,
        out_shape=jax.ShapeDtypeStruct(xc.shape, F32),
        compiler_params=pltpu.CompilerParams(vmem_limit_bytes=V7X_VMEM_LIMIT),
        name="cumsum",
    )(xc)
    return out.transpose(1, 0, 2).reshape(rows, length)


def _suffix_ones(n):
    return (lax.broadcasted_iota(jnp.int32, (n, n), 0)
            >= lax.broadcasted_iota(jnp.int32, (n, n), 1)).astype(BF16)


def _sb_step(qs, k, v, suffix, states, masks):
    live = [c for c, m in enumerate(masks) if not isinstance(m, str)]
    z = {c: _dot_nt(qs[c], k) for c in live}
    sp = {}
    for c in live:
        s = _softplus2(z[c])
        sp[c] = s if masks[c] is None else jnp.where(masks[c], s, 0.0)
    incl = {c: _dot(sp[c].astype(BF16), suffix) for c in live}
    out = list(states)
    for c in live:
        carry, acc = states[c]
        a = jnp.exp2(z[c] - incl[c] - carry)
        if masks[c] is not None:
            a = jnp.where(masks[c], a, 0.0)
        out[c] = (carry + incl[c][:, :1], acc + _dot(a.astype(BF16), v))
    return out


def _fox_step(qs, k, v, biases, states, masks):
    live = [c for c, m in enumerate(masks) if not isinstance(m, str)]
    s = {}
    for c in live:
        sc = _dot_nt(qs[c], k) + biases[c]
        s[c] = sc if masks[c] is None else jnp.where(masks[c], sc, -jnp.inf)
    out = list(states)
    for c in live:
        m, l, acc = states[c]
        m_new = jnp.maximum(m, jnp.max(s[c], axis=-1, keepdims=True))
        alpha = jnp.exp2(m - m_new)
        p = jnp.exp2(s[c] - m_new)
        out[c] = (m_new, alpha * l + jnp.sum(p, axis=-1, keepdims=True), alpha * acc + _dot(p.astype(BF16), v))
    return out


def _diag_masks(jj, tk, rc, n_chains, strict):
    masks = []
    for c in range(n_chains):
        lo_row, hi_row = c * rc, (c + 1) * rc - 1
        lo_key, hi_key = jj * tk, (jj + 1) * tk - 1
        if (lo_key >= hi_row) if strict else (lo_key > hi_row):
            masks.append("skip")
        elif (hi_key < lo_row) if strict else (hi_key <= lo_row):
            masks.append(None)
        else:
            rows = lax.broadcasted_iota(jnp.int32, (rc, tk), 0) + lo_row
            cols = lax.broadcasted_iota(jnp.int32, (rc, tk), 1) + lo_key
            masks.append(cols < rows if strict else cols <= rows)
    return masks


def _head_norm(o, g):
    return (o * lax.rsqrt(jnp.mean(o * o, axis=-1, keepdims=True) + EPS) * g).astype(BF16)


def _sb_prompt_kernel(q_ref, k_ref, v_ref, g_ref, o_ref, z_ref, a_ref, acc_ref, *, tq, tk, n_chains):
    qi = pl.program_id(1)
    rc = tq // n_chains
    ratio = tq // tk
    qs = [q_ref[0, 0, c * rc:(c + 1) * rc, :] for c in range(n_chains)]
    suffix = _suffix_ones(tk)

    def kv(j):
        start = pl.multiple_of(j * tk, tk)
        return k_ref[0, 0, pl.ds(start, tk), :], v_ref[0, 0, pl.ds(start, tk), :]

    states = [(jnp.zeros((rc, 1), F32), jnp.zeros((rc, HEAD_DIM), F32)) for _ in range(n_chains)]
    for jj in reversed(range(ratio)):
        k, v = kv(qi * ratio + jj)
        states = _sb_step(qs, k, v, suffix, states, _diag_masks(jj, tk, rc, n_chains, True))

    n_main = qi * ratio
    chains = range(n_chains)

    def put_scores(slot, j):
        k, _ = kv(jnp.maximum(j, 0))
        for c in chains:
            z_ref[slot, c] = _dot_nt(qs[c], k)

    def half_trip(src, dst, j, j_prev, carries):
        put_scores(dst, j - 1)
        _, v_prev = kv(j_prev)
        for c in chains:
            acc_ref[c] += _dot(a_ref[src, c], v_prev)
        z = [z_ref[src, c] for c in chains]
        incl = [_dot(_softplus2(z[c]).astype(BF16), suffix) for c in chains]
        for c in chains:
            a_ref[dst, c] = jnp.exp2(z[c] - incl[c] - carries[c]).astype(BF16)
        return [carries[c] + incl[c][:, :1] for c in chains]

    def body(pair, loop_state):
        j_prev, carries = loop_state
        j = n_main - 1 - 2 * pair
        carries = half_trip(0, 1, j, j_prev, carries)
        carries = half_trip(1, 0, j - 1, j, carries)
        return j - 1, carries

    put_scores(0, n_main - 1)
    for c in chains:
        a_ref[0, c] = jnp.zeros((rc, tk), BF16)
        acc_ref[c] = states[c][1]
    j_last, _ = lax.fori_loop(0, n_main // 2, body, (0, [states[c][0] for c in chains]))
    _, v_last = kv(j_last)
    for c in chains:
        o_ref[c * rc:(c + 1) * rc, :] = _head_norm(acc_ref[c] + _dot(a_ref[0, c], v_last), g_ref[0])


def _fox_prompt_kernel(q_ref, k_ref, v_ref, cq_ref, ck_ref, g_ref, o_ref, z_ref, p_ref, acc_ref, *, tq, tk, n_chains):
    qi = pl.program_id(1)
    rc = tq // n_chains
    ratio = tq // tk
    qs = [q_ref[0, 0, c * rc:(c + 1) * rc, :] for c in range(n_chains)]
    cqs = [cq_ref[0, c * rc:(c + 1) * rc, :] for c in range(n_chains)]

    def kv(j):
        start = pl.multiple_of(j * tk, tk)
        return k_ref[0, 0, pl.ds(start, tk), :], v_ref[0, 0, pl.ds(start, tk), :]

    states = [(jnp.full((rc, 1), -jnp.inf, F32), jnp.zeros((rc, 1), F32), jnp.zeros((rc, HEAD_DIM), F32))
              for _ in range(n_chains)]
    for jj in range(ratio):
        k, v = kv(qi * ratio + jj)
        ck = ck_ref[0, qi * ratio + jj]
        states = _fox_step(qs, k, v, [cq - ck for cq in cqs], states, _diag_masks(jj, tk, rc, n_chains, False))

    n_main = qi * ratio
    chains = range(n_chains)

    def put_scores(slot, j):
        k, _ = kv(jnp.maximum(j, 0))
        for c in chains:
            z_ref[slot, c] = _dot_nt(qs[c], k)

    lane = lax.broadcasted_iota(jnp.int32, (1, HEAD_DIM), 1)
    ones_col = jnp.broadcast_to(jnp.where(lane == 0, 1.0, 0.0).astype(BF16), (tk, HEAD_DIM))

    def values(j):
        _, v = kv(j)
        return jnp.concatenate([v, ones_col], axis=1)

    def half_trip(src, dst, j, j_prev, maxes):
        put_scores(dst, j - 1)
        v_prev = values(j_prev)
        ck = ck_ref[0, j]
        out = []
        for c in chains:
            s = z_ref[src, c] + (cqs[c] - ck)
            m_new = jnp.maximum(maxes[c], jnp.max(s, axis=-1, keepdims=True))
            acc_ref[c] = (acc_ref[c] + _dot(p_ref[src, c], v_prev)) * jnp.exp2(maxes[c] - m_new)
            p_ref[dst, c] = jnp.exp2(s - m_new).astype(BF16)
            out.append(m_new)
        return out

    def body(pair, loop_state):
        j_prev, maxes = loop_state
        j = n_main - 1 - 2 * pair
        maxes = half_trip(0, 1, j, j_prev, maxes)
        maxes = half_trip(1, 0, j - 1, j, maxes)
        return j - 1, maxes

    put_scores(0, n_main - 1)
    for c in chains:
        p_ref[0, c] = jnp.zeros((rc, tk), BF16)
        acc_ref[c] = jnp.concatenate([states[c][2], jnp.where(lane == 0, states[c][1], 0.0)], axis=1)
    j_last, _ = lax.fori_loop(0, n_main // 2, body, (0, [states[c][0] for c in chains]))
    v_last = values(j_last)
    for c in chains:
        acc = acc_ref[c] + _dot(p_ref[0, c], v_last)
        o_ref[c * rc:(c + 1) * rc, :] = _head_norm(acc[:, :HEAD_DIM] / acc[:, HEAD_DIM:HEAD_DIM + 1], g_ref[0])


ATTN_TQ, ATTN_TK, ATTN_CHAINS = 512, 256, 2


def _prompt_attention(qkv, c_rows, g_sb, g_fx):
    t = qkv.shape[2]
    tq, tk = min(ATTN_TQ, t), min(ATTN_TK, t)
    cfg = dict(tq=tq, tk=tk, n_chains=ATTN_CHAINS)
    rc = tq // ATTN_CHAINS
    assert (tq // tk) % 2 == 0, "the pipelined sweep walks key blocks in pairs"
    scratch = [pltpu.VMEM((2, ATTN_CHAINS, rc, tk), F32), pltpu.VMEM((2, ATTN_CHAINS, rc, tk), BF16),
               pltpu.VMEM((ATTN_CHAINS, rc, HEAD_DIM), F32)]
    q_spec = lambda n: pl.BlockSpec((1, 1, tq, HEAD_DIM), lambda h, i: (n, h, i, 0))
    kv_spec = lambda n: pl.BlockSpec((1, 1, t, HEAD_DIM), lambda h, i: (n, h, 0, 0))
    g_spec = pl.BlockSpec((1, 1, HEAD_DIM), lambda h, i: (h, 0, 0))
    o_spec = pl.BlockSpec((tq, HEAD_DIM), lambda h, i: (i, h))
    o_shape = jax.ShapeDtypeStruct((t, D_GROUP), BF16)
    o_sb = pl.pallas_call(
        functools.partial(_sb_prompt_kernel, **cfg),
        grid=(N_HEADS, t // tq),
        in_specs=[q_spec(0), kv_spec(1), kv_spec(2), g_spec],
        out_specs=o_spec, out_shape=o_shape, scratch_shapes=scratch,
        compiler_params=_params(("parallel", "arbitrary")),
        name="sb_prompt",
    )(qkv, qkv, qkv, g_sb)
    cq = c_rows.reshape(N_HEADS, t, 1)
    ck = c_rows.reshape(N_HEADS, t // tk, 1, tk)
    o_fx = pl.pallas_call(
        functools.partial(_fox_prompt_kernel, **cfg),
        grid=(N_HEADS, t // tq),
        in_specs=[q_spec(3), kv_spec(4), kv_spec(5),
                  pl.BlockSpec((1, tq, 1), lambda h, i: (h, i, 0)),
                  pl.BlockSpec((1, t // tk, 1, tk), lambda h, i: (h, 0, 0, 0)),
                  g_spec],
        out_specs=o_spec, out_shape=o_shape,
        scratch_shapes=scratch[:2] + [pltpu.VMEM((ATTN_CHAINS, rc, 2 * HEAD_DIM), F32)],
        compiler_params=_params(("parallel", "arbitrary")),
        name="fox_prompt",
    )(qkv, qkv, qkv, cq, ck, g_fx)
    return o_sb, o_fx


def _sb_sample_kernel(q_ref, k_ref, v_ref, pk_ref, pv_ref, g_ref, o_ref, *, blk):
    tq = q_ref.shape[3]
    past = pk_ref.shape[3]
    q = q_ref[0, 0, 0]
    states = [(jnp.zeros((tq, 1), F32), jnp.zeros((tq, HEAD_DIM), F32))]
    (carry, acc), = _sb_step([q], k_ref[0, 0, 0], v_ref[0, 0, 0], _suffix_ones(tq), states,
                             _diag_masks(0, tq, tq, 1, True))
    suffix = _suffix_ones(blk)
    blocks = range(past // blk)
    k_blk = lambda j: pk_ref[0, 0, 0, j * blk:(j + 1) * blk, :].astype(BF16)
    v_blk = lambda j: pv_ref[0, 0, 0, j * blk:(j + 1) * blk, :].astype(BF16)
    z = [_dot_nt(q, k_blk(j)) for j in blocks]
    incl = [_dot(_softplus2(z[j]).astype(BF16), suffix) for j in blocks]
    for j in reversed(blocks):
        acc = acc + _dot(jnp.exp2(z[j] - incl[j] - carry).astype(BF16), v_blk(j))
        carry = carry + incl[j][:, :1]
    o_ref[...] = _head_norm(acc, g_ref[0])


def _fox_sample_kernel(q_ref, k_ref, v_ref, pk_ref, pv_ref, cq_ref, cn_ref, cp_ref, g_ref, o_ref, *, blk):
    tq = q_ref.shape[3]
    past = pk_ref.shape[3]
    q = q_ref[0, 0, 0]
    cq = cq_ref[0, 0]
    blocks = range(past // blk)
    k_blk = lambda j: pk_ref[0, 0, 0, j * blk:(j + 1) * blk, :].astype(BF16)
    v_blk = lambda j: pv_ref[0, 0, 0, j * blk:(j + 1) * blk, :].astype(BF16)
    (mask,) = _diag_masks(0, tq, tq, 1, False)
    s_new = jnp.where(mask, _dot_nt(q, k_ref[0, 0, 0]) + (cq - cn_ref[0, 0]), -jnp.inf)
    s = [_dot_nt(q, k_blk(j)) + (cq - cp_ref[0, 0, j]) for j in blocks]
    m = jnp.max(s_new, axis=-1, keepdims=True)
    for j in blocks:
        m = jnp.maximum(m, jnp.max(s[j], axis=-1, keepdims=True))
    p_new = jnp.exp2(s_new - m)
    l = jnp.sum(p_new, axis=-1, keepdims=True)
    acc = _dot(p_new.astype(BF16), v_ref[0, 0, 0])
    for j in blocks:
        p = jnp.exp2(s[j] - m)
        l = l + jnp.sum(p, axis=-1, keepdims=True)
        acc = acc + _dot(p.astype(BF16), v_blk(j))
    o_ref[...] = _head_norm(acc / l, g_ref[0])


def _sample_attention(qkv, past_sb_k, past_sb_v, past_fx_k, past_fx_v, c_all, g_sb, g_fx, blk):
    b, _, past, _ = past_sb_k.shape
    tq = qkv.shape[2] // b
    qkv = qkv.reshape(6, N_HEADS, b, tq, HEAD_DIM)
    new_spec = lambda n: pl.BlockSpec((1, 1, 1, tq, HEAD_DIM), lambda bi, h: (n, h, bi, 0, 0))
    past_spec = pl.BlockSpec((1, 1, 1, past, HEAD_DIM), lambda bi, h: (0, bi, h, 0, 0))
    g_spec = pl.BlockSpec((1, 1, HEAD_DIM), lambda bi, h: (h, 0, 0))
    o_spec = pl.BlockSpec((tq, HEAD_DIM), lambda bi, h: (bi, h))
    o_shape = jax.ShapeDtypeStruct((b * tq, D_GROUP), BF16)
    o_sb = pl.pallas_call(
        functools.partial(_sb_sample_kernel, blk=blk),
        grid=(b, N_HEADS),
        in_specs=[new_spec(0), new_spec(1), new_spec(2), past_spec, past_spec, g_spec],
        out_specs=o_spec, out_shape=o_shape,
        compiler_params=_params(("parallel", "parallel")),
        name="sb_sample",
    )(qkv, qkv, qkv, past_sb_k[None], past_sb_v[None], g_sb)
    n_blk = past // blk
    cq = c_all[:, :, past:].reshape(b, N_HEADS, tq, 1)
    cn = c_all[:, :, past:].reshape(b, N_HEADS, 1, tq)
    cp = c_all[:, :, :past].reshape(b, N_HEADS, n_blk, 1, blk)
    o_fx = pl.pallas_call(
        functools.partial(_fox_sample_kernel, blk=blk),
        grid=(b, N_HEADS),
        in_specs=[new_spec(3), new_spec(4), new_spec(5), past_spec, past_spec,
                  pl.BlockSpec((1, 1, tq, 1), lambda bi, h: (bi, h, 0, 0)),
                  pl.BlockSpec((1, 1, 1, tq), lambda bi, h: (bi, h, 0, 0)),
                  pl.BlockSpec((1, 1, n_blk, 1, blk), lambda bi, h: (bi, h, 0, 0, 0)),
                  g_spec],
        out_specs=o_spec, out_shape=o_shape,
        compiler_params=_params(("parallel", "parallel")),
        name="fox_sample",
    )(qkv, qkv, qkv, past_fx_k[None], past_fx_v[None], cq, cn, cp, g_fx)
    return o_sb, o_fx


def _out_proj_kernel(x_ref, osb_ref, ofx_ref, wsb_ref, wfx_ref, o_ref):
    o_ref[...] = x_ref[...] + _dot(osb_ref[...], wsb_ref[...]) + _dot(ofx_ref[...], wfx_ref[...])


def _out_proj(x, o_sb, o_fx, w_sb, w_fx, tm):
    m, d = x.shape
    row = lambda i: (i, 0)
    full = lambda i: (0, 0)
    return pl.pallas_call(
        _out_proj_kernel,
        grid=(m // tm,),
        in_specs=[pl.BlockSpec((tm, d), row), pl.BlockSpec((tm, D_GROUP), row), pl.BlockSpec((tm, D_GROUP), row),
                  pl.BlockSpec((D_GROUP, d), full), pl.BlockSpec((D_GROUP, d), full)],
        out_specs=pl.BlockSpec((tm, d), row),
        out_shape=jax.ShapeDtypeStruct((m, d), F32),
        compiler_params=_params(("parallel",)),
        name="out_proj",
    )(x, o_sb, o_fx, w_sb, w_fx)


def _ple_kernel(x_ref, xn_ref, p_ref, wg_ref, wp_ref, gf_ref, o_ref):
    gate = jax.nn.sigmoid(_dot(xn_ref[...], wg_ref[...]))
    emb = _dot(p_ref[...].astype(BF16), wp_ref[...])
    o_ref[...] = _rms(x_ref[...] + gate * emb, gf_ref[...])


def _ple_final(x, xn, p, w_gate, w_proj, g_final, tm):
    m, d = x.shape
    ple = p.shape[1]
    row = lambda i: (i, 0)
    full = lambda i: (0, 0)
    return pl.pallas_call(
        _ple_kernel,
        grid=(m // tm,),
        in_specs=[pl.BlockSpec((tm, d), row), pl.BlockSpec((tm, d), row), pl.BlockSpec((tm, ple), row),
                  pl.BlockSpec((d, d), full), pl.BlockSpec((ple, d), full), pl.BlockSpec((1, d), full)],
        out_specs=pl.BlockSpec((tm, d), row),
        out_shape=jax.ShapeDtypeStruct((m, d), F32),
        compiler_params=_params(("parallel",)),
        name="ple_final",
    )(x, xn, p, w_gate, w_proj, g_final)


def _pad_lanes(x, n):
    return jnp.pad(x, ((0, 0), (0, n - x.shape[1])))


def _run_group(x, p, past, w, attn_blk):
    b, t, d = x.shape
    m = b * t
    tm = min(512, m)
    row = lambda g: g.reshape(1, -1)
    x0 = x.reshape(m, d)
    x1, xn = _ffn(x0, row(w["g_ffn1"]), w["ffn1_gate"], w["ffn1_up"], w["ffn1_down"], row(w["g_mix"]), tm, 512)
    qkv, k_sb, v_sb, k_fx, v_fx, lf_cols = _proj(xn, w["in_qkv"], w["in_f"], w["b_f"], tm)
    lf = lf_cols[:, :N_HEADS].reshape(b, t, N_HEADS).transpose(0, 2, 1)
    g_sb = w["g_out_sb"].reshape(N_HEADS, 1, HEAD_DIM)
    g_fx = w["g_out_fox"].reshape(N_HEADS, 1, HEAD_DIM)
    if past is None:
        c_rows = _cumsum_rows(lf.reshape(N_HEADS, t))
        o_sb, o_fx = _prompt_attention(qkv, c_rows, g_sb, g_fx)
    else:
        lf_all = jnp.concatenate([past[4], lf], axis=-1).reshape(b * N_HEADS, -1)
        total = lf_all.shape[1]
        padded = -(-total // LANES) * LANES
        c_all = _cumsum_rows(_pad_lanes(lf_all, padded))[:, :total].reshape(b, N_HEADS, total)
        o_sb, o_fx = _sample_attention(qkv, past[0], past[1], past[2], past[3], c_all, g_sb, g_fx, attn_blk)
    x2 = _out_proj(x1, o_sb, o_fx, w["out_sb"], w["out_fx"], tm)
    x3, xn3 = _ffn(x2, row(w["g_ffn2"]), w["ffn2_gate"], w["ffn2_up"], w["ffn2_down"], row(w["g_ple"]), tm, 512)
    y = _ple_final(x3, xn3, p.reshape(m, -1), w["ple_gate"], w["ple_proj"], row(w["g_final"]), min(256, m))
    heads = lambda a: a.reshape(N_HEADS, b, t, HEAD_DIM).transpose(1, 0, 2, 3)[None]
    return y.reshape(b, t, d), heads(k_sb), heads(v_sb), heads(k_fx), heads(v_fx), lf[None]


def kernel(x_prompt, x_sample, cache_sb_k, cache_sb_v, cache_fox_k, cache_fox_v, cache_fox_logf, p_prompt, p_sample, g_ffn1, w_ffn1_gate, w_ffn1_up, w_ffn1_down, g_mix, w_in, b_forget, g_out_sb, g_out_fox, w_out, g_ffn2, w_ffn2_gate, w_ffn2_up, w_ffn2_down, g_ple, w_ple_gate, w_ple_proj, g_final):
    assert g_ffn1.shape[0] == 1, "single-layer trunk"
    bf = lambda a: a.astype(BF16)
    n_qkv = 6 * D_GROUP
    w = {
        "g_ffn1": g_ffn1[0], "ffn1_gate": bf(w_ffn1_gate[0]), "ffn1_up": bf(w_ffn1_up[0]), "ffn1_down": bf(w_ffn1_down[0]),
        "g_mix": g_mix[0], "in_qkv": bf(w_in[0][:, :n_qkv]),
        "in_f": _pad_lanes(bf(w_in[0][:, n_qkv:]), LANES), "b_f": _pad_lanes(b_forget[0][None, :], LANES),
        "g_out_sb": g_out_sb[0], "g_out_fox": g_out_fox[0],
        "out_sb": bf(w_out[0][:D_GROUP]), "out_fx": bf(w_out[0][D_GROUP:]),
        "g_ffn2": g_ffn2[0], "ffn2_gate": bf(w_ffn2_gate[0]), "ffn2_up": bf(w_ffn2_up[0]), "ffn2_down": bf(w_ffn2_down[0]),
        "g_ple": g_ple[0], "ple_gate": bf(w_ple_gate[0]), "ple_proj": bf(w_ple_proj[0]), "g_final": g_final,
    }
    past = (cache_sb_k[0], cache_sb_v[0], cache_fox_k[0], cache_fox_v[0], cache_fox_logf[0])
    y_p, sb_k_p, sb_v_p, fx_k_p, fx_v_p, lf_p = _run_group(x_prompt, p_prompt[0], None, w, 256)
    y_s, sb_k_s, sb_v_s, fx_k_s, fx_v_s, lf_s = _run_group(x_sample, p_sample[0], past, w, 256)
    return (y_p, y_s, sb_k_p, sb_v_p, fx_k_p, fx_v_p, lf_p, sb_k_s, sb_v_s, fx_k_s, fx_v_s, lf_s)
```

```python
import functools
import math

import jax
import jax.numpy as jnp
from jax import lax
from jax.experimental import pallas as pl
from jax.experimental.pallas import tpu as pltpu

HEAD_DIM = 128
N_HEADS = 8
D_GROUP = N_HEADS * HEAD_DIM
MACARON = 0.5
EPS = 1e-6
SCALE = 1.0 / math.sqrt(HEAD_DIM)
LOG2E = math.log2(math.e)
LANES = 128
V7X_VMEM_LIMIT = 56 * 1024 * 1024

F32 = jnp.float32
BF16 = jnp.bfloat16


def _params(semantics):
    return pltpu.CompilerParams(dimension_semantics=semantics, vmem_limit_bytes=V7X_VMEM_LIMIT)


def _rms(x, g):
    return x * lax.rsqrt(jnp.mean(x * x, axis=-1, keepdims=True) + EPS) * g


def _dot(a, b):
    return jnp.dot(a, b, preferred_element_type=F32)


def _dot_nt(a, b):
    return lax.dot_general(a, b, (((1,), (1,)), ((), ())), preferred_element_type=F32)


def _split3(x):
    hi = x.astype(BF16)
    r = x - hi.astype(F32)
    mid = r.astype(BF16)
    lo = (r - mid.astype(F32)).astype(BF16)
    return hi, mid, lo


def _softplus2(z2):
    return jnp.maximum(z2, 0.0) + jnp.log(1.0 + jnp.exp2(-jnp.abs(z2))) * LOG2E


def _ffn_kernel(x_ref, g_ref, wg_ref, wu_ref, wd_ref, gn_ref, o_ref, on_ref, xn_ref):
    f = pl.program_id(1)

    @pl.when(f == 0)
    def _():
        xn_ref[...] = _rms(x_ref[...], g_ref[...]).astype(BF16)
        o_ref[...] = jnp.zeros_like(o_ref)

    xn = xn_ref[...]
    gate = _dot(xn, wg_ref[...])
    up = _dot(xn, wu_ref[...])
    h = (gate * jax.nn.sigmoid(gate) * up).astype(BF16)
    o_ref[...] += _dot(h, wd_ref[...])

    @pl.when(f == pl.num_programs(1) - 1)
    def _():
        y = x_ref[...] + MACARON * o_ref[...]
        o_ref[...] = y
        on_ref[...] = _rms(y, gn_ref[...]).astype(BF16)


def _ffn(x, g, wg, wu, wd, g_next, tm, tf):
    m, d = x.shape
    d_ff = wg.shape[1]
    row = lambda i, f: (i, 0)
    return pl.pallas_call(
        _ffn_kernel,
        grid=(m // tm, d_ff // tf),
        in_specs=[
            pl.BlockSpec((tm, d), row),
            pl.BlockSpec((1, d), lambda i, f: (0, 0)),
            pl.BlockSpec((d, tf), lambda i, f: (0, f)),
            pl.BlockSpec((d, tf), lambda i, f: (0, f)),
            pl.BlockSpec((tf, d), lambda i, f: (f, 0)),
            pl.BlockSpec((1, d), lambda i, f: (0, 0)),
        ],
        out_specs=[pl.BlockSpec((tm, d), row), pl.BlockSpec((tm, d), row)],
        out_shape=[jax.ShapeDtypeStruct((m, d), F32), jax.ShapeDtypeStruct((m, d), BF16)],
        scratch_shapes=[pltpu.VMEM((tm, d), BF16)],
        compiler_params=_params(("parallel", "arbitrary")),
        name="ffn",
    )(x, g, wg, wu, wd, g_next)


def _proj_kernel(xn_ref, w_ref, wf_ref, bf_ref, qkv_ref, ksb_ref, vsb_ref, kfx_ref, vfx_ref, lf_ref):
    n = pl.program_id(1)
    xn = xn_ref[...]
    res = _dot(xn, w_ref[...])
    is_q = jnp.logical_or(n == 0, n == 3)
    res_mx = res * jnp.where(is_q, SCALE * LOG2E, 1.0)
    for h in range(N_HEADS):
        qkv_ref[0, h] = res_mx[:, h * HEAD_DIM:(h + 1) * HEAD_DIM].astype(BF16)

    def store_heads(ref):
        for h in range(N_HEADS):
            ref[h] = res[:, h * HEAD_DIM:(h + 1) * HEAD_DIM]

    for idx, ref in ((1, ksb_ref), (2, vsb_ref), (4, kfx_ref), (5, vfx_ref)):
        pl.when(n == idx)(functools.partial(store_heads, ref))

    @pl.when(n == 0)
    def _():
        fz = _dot(xn, wf_ref[...]) + bf_ref[...]
        lf_ref[...] = jnp.minimum(fz, 0.0) - jnp.log1p(jnp.exp(-jnp.abs(fz)))


def _proj(xn, w_qkv, w_f, b_f, tm):
    m, d = xn.shape
    head_f32 = jax.ShapeDtypeStruct((N_HEADS, m, HEAD_DIM), F32)
    head_spec = pl.BlockSpec((N_HEADS, tm, HEAD_DIM), lambda i, n: (0, i, 0))
    return pl.pallas_call(
        _proj_kernel,
        grid=(m // tm, 6),
        in_specs=[
            pl.BlockSpec((tm, d), lambda i, n: (i, 0)),
            pl.BlockSpec((d, D_GROUP), lambda i, n: (0, n)),
            pl.BlockSpec((d, LANES), lambda i, n: (0, 0)),
            pl.BlockSpec((1, LANES), lambda i, n: (0, 0)),
        ],
        out_specs=[
            pl.BlockSpec((1, N_HEADS, tm, HEAD_DIM), lambda i, n: (n, 0, i, 0)),
            head_spec, head_spec, head_spec, head_spec,
            pl.BlockSpec((tm, LANES), lambda i, n: (i, 0)),
        ],
        out_shape=[
            jax.ShapeDtypeStruct((6, N_HEADS, m, HEAD_DIM), BF16),
            head_f32, head_f32, head_f32, head_f32,
            jax.ShapeDtypeStruct((m, LANES), F32),
        ],
        compiler_params=_params(("parallel", "arbitrary")),
        name="in_proj",
    )(xn, w_qkv, w_f, b_f)


def _cumsum_kernel(x_ref, o_ref):
    n_chunks, rows, _ = x_ref.shape
    upper = (lax.broadcasted_iota(jnp.int32, (LANES, LANES), 0)
             <= lax.broadcasted_iota(jnp.int32, (LANES, LANES), 1)).astype(BF16)

    def body(c, carry):
        hi, mid, lo = _split3(x_ref[c])
        cs = _dot(hi, upper) + _dot(mid, upper) + _dot(lo, upper) + carry
        o_ref[c] = cs * LOG2E
        return cs[:, LANES - 1:LANES]

    lax.fori_loop(0, n_chunks, body, jnp.zeros((rows, 1), F32))


def _cumsum_rows(x):
    rows, length = x.shape
    xc = x.reshape(rows, length // LANES, LANES).transpose(1, 0, 2)
    out = pl.pallas_call(
        _cumsum_kernel,
        out_shape=jax.ShapeDtypeStruct(xc.shape, F32),
        compiler_params=pltpu.CompilerParams(vmem_limit_bytes=V7X_VMEM_LIMIT),
        name="cumsum",
    )(xc)
    return out.transpose(1, 0, 2).reshape(rows, length)


def _suffix_ones(n):
    return (lax.broadcasted_iota(jnp.int32, (n, n), 0)
            >= lax.broadcasted_iota(jnp.int32, (n, n), 1)).astype(BF16)


def _sb_step(qs, k, v, suffix, states, masks):
    live = [c for c, m in enumerate(masks) if not isinstance(m, str)]
    z = {c: _dot_nt(qs[c], k) for c in live}
    sp = {}
    for c in live:
        s = _softplus2(z[c])
        sp[c] = s if masks[c] is None else jnp.where(masks[c], s, 0.0)
    incl = {c: _dot(sp[c].astype(BF16), suffix) for c in live}
    out = list(states)
    for c in live:
        carry, acc = states[c]
        a = jnp.exp2(z[c] - incl[c] - carry)
        if masks[c] is not None:
            a = jnp.where(masks[c], a, 0.0)
        out[c] = (carry + incl[c][:, :1], acc + _dot(a.astype(BF16), v))
    return out


def _fox_step(qs, k, v, biases, states, masks):
    live = [c for c, m in enumerate(masks) if not isinstance(m, str)]
    s = {}
    for c in live:
        sc = _dot_nt(qs[c], k) + biases[c]
        s[c] = sc if masks[c] is None else jnp.where(masks[c], sc, -jnp.inf)
    out = list(states)
    for c in live:
        m, l, acc = states[c]
        m_new = jnp.maximum(m, jnp.max(s[c], axis=-1, keepdims=True))
        alpha = jnp.exp2(m - m_new)
        p = jnp.exp2(s[c] - m_new)
        out[c] = (m_new, alpha * l + jnp.sum(p, axis=-1, keepdims=True), alpha * acc + _dot(p.astype(BF16), v))
    return out


def _diag_masks(jj, tk, rc, n_chains, strict):
    masks = []
    for c in range(n_chains):
        lo_row, hi_row = c * rc, (c + 1) * rc - 1
        lo_key, hi_key = jj * tk, (jj + 1) * tk - 1
        if (lo_key >= hi_row) if strict else (lo_key > hi_row):
            masks.append("skip")
        elif (hi_key < lo_row) if strict else (hi_key <= lo_row):
            masks.append(None)
        else:
            rows = lax.broadcasted_iota(jnp.int32, (rc, tk), 0) + lo_row
            cols = lax.broadcasted_iota(jnp.int32, (rc, tk), 1) + lo_key
            masks.append(cols < rows if strict else cols <= rows)
    return masks


def _head_norm(o, g):
    return (o * lax.rsqrt(jnp.mean(o * o, axis=-1, keepdims=True) + EPS) * g).astype(BF16)


def _sb_prompt_kernel(q_ref, k_ref, v_ref, g_ref, o_ref, z_ref, a_ref, acc_ref, *, tq, tk, n_chains):
    qi = pl.program_id(1)
    rc = tq // n_chains
    ratio = tq // tk
    qs = [q_ref[0, 0, c * rc:(c + 1) * rc, :] for c in range(n_chains)]
    suffix = _suffix_ones(tk)

    def kv(j):
        start = pl.multiple_of(j * tk, tk)
        return k_ref[0, 0, pl.ds(start, tk), :], v_ref[0, 0, pl.ds(start, tk), :]

    states = [(jnp.zeros((rc, 1), F32), jnp.zeros((rc, HEAD_DIM), F32)) for _ in range(n_chains)]
    for jj in reversed(range(ratio)):
        k, v = kv(qi * ratio + jj)
        states = _sb_step(qs, k, v, suffix, states, _diag_masks(jj, tk, rc, n_chains, True))

    n_main = qi * ratio
    chains = range(n_chains)

    def put_scores(slot, j):
        k, _ = kv(jnp.maximum(j, 0))
        for c in chains:
            z_ref[slot, c] = _dot_nt(qs[c], k)

    def half_trip(src, dst, j, j_prev, carries):
        put_scores(dst, j - 1)
        _, v_prev = kv(j_prev)
        for c in chains:
            acc_ref[c] += _dot(a_ref[src, c], v_prev)
        z = [z_ref[src, c] for c in chains]
        incl = [_dot(_softplus2(z[c]).astype(BF16), suffix) for c in chains]
        for c in chains:
            a_ref[dst, c] = jnp.exp2(z[c] - incl[c] - carries[c]).astype(BF16)
        return [carries[c] + incl[c][:, :1] for c in chains]

    def body(pair, loop_state):
        j_prev, carries = loop_state
        j = n_main - 1 - 2 * pair
        carries = half_trip(0, 1, j, j_prev, carries)
        carries = half_trip(1, 0, j - 1, j, carries)
        return j - 1, carries

    put_scores(0, n_main - 1)
    for c in chains:
        a_ref[0, c] = jnp.zeros((rc, tk), BF16)
        acc_ref[c] = states[c][1]
    j_last, _ = lax.fori_loop(0, n_main // 2, body, (0, [states[c][0] for c in chains]))
    _, v_last = kv(j_last)
    for c in chains:
        o_ref[c * rc:(c + 1) * rc, :] = _head_norm(acc_ref[c] + _dot(a_ref[0, c], v_last), g_ref[0])


def _fox_prompt_kernel(q_ref, k_ref, v_ref, cq_ref, ck_ref, g_ref, o_ref, z_ref, p_ref, acc_ref, *, tq, tk, n_chains):
    qi = pl.program_id(1)
    rc = tq // n_chains
    ratio = tq // tk
    qs = [q_ref[0, 0, c * rc:(c + 1) * rc, :] for c in range(n_chains)]
    cqs = [cq_ref[0, c * rc:(c + 1) * rc, :] for c in range(n_chains)]

    def kv(j):
        start = pl.multiple_of(j * tk, tk)
        return k_ref[0, 0, pl.ds(start, tk), :], v_ref[0, 0, pl.ds(start, tk), :]

    states = [(jnp.full((rc, 1), -jnp.inf, F32), jnp.zeros((rc, 1), F32), jnp.zeros((rc, HEAD_DIM), F32))
              for _ in range(n_chains)]
    for jj in range(ratio):
        k, v = kv(qi * ratio + jj)
        ck = ck_ref[0, qi * ratio + jj]
        states = _fox_step(qs, k, v, [cq - ck for cq in cqs], states, _diag_masks(jj, tk, rc, n_chains, False))

    n_main = qi * ratio
    chains = range(n_chains)

    def put_scores(slot, j):
        k, _ = kv(jnp.maximum(j, 0))
        for c in chains:
            z_ref[slot, c] = _dot_nt(qs[c], k)

    lane = lax.broadcasted_iota(jnp.int32, (1, HEAD_DIM), 1)
    ones_col = jnp.broadcast_to(jnp.where(lane == 0, 1.0, 0.0).astype(BF16), (tk, HEAD_DIM))

    def values(j):
        _, v = kv(j)
        return jnp.concatenate([v, ones_col], axis=1)

    def half_trip(src, dst, j, j_prev, maxes):
        put_scores(dst, j - 1)
        v_prev = values(j_prev)
        ck = ck_ref[0, j]
        out = []
        for c in chains:
            s = z_ref[src, c] + (cqs[c] - ck)
            m_new = jnp.maximum(maxes[c], jnp.max(s, axis=-1, keepdims=True))
            acc_ref[c] = (acc_ref[c] + _dot(p_ref[src, c], v_prev)) * jnp.exp2(maxes[c] - m_new)
            p_ref[dst, c] = jnp.exp2(s - m_new).astype(BF16)
            out.append(m_new)
        return out

    def body(pair, loop_state):
        j_prev, maxes = loop_state
        j = n_main - 1 - 2 * pair
        maxes = half_trip(0, 1, j, j_prev, maxes)
        maxes = half_trip(1, 0, j - 1, j, maxes)
        return j - 1, maxes

    put_scores(0, n_main - 1)
    for c in chains:
        p_ref[0, c] = jnp.zeros((rc, tk), BF16)
        acc_ref[c] = jnp.concatenate([states[c][2], jnp.where(lane == 0, states[c][1], 0.0)], axis=1)
    j_last, _ = lax.fori_loop(0, n_main // 2, body, (0, [states[c][0] for c in chains]))
    v_last = values(j_last)
    for c in chains:
        acc = acc_ref[c] + _dot(p_ref[0, c], v_last)
        o_ref[c * rc:(c + 1) * rc, :] = _head_norm(acc[:, :HEAD_DIM] / acc[:, HEAD_DIM:HEAD_DIM + 1], g_ref[0])


SB_TILES = (512, 256, 2)
FOX_TILES = (1024, 512, 2)


def _attn_call(kernel_fn, name, tiles, t, acc_lanes, extra_specs):
    tq, tk, n_chains = (min(tiles[0], t), min(tiles[1], t), tiles[2])
    rc = tq // n_chains
    assert (tq // tk) % 2 == 0, "the pipelined sweep walks key blocks in pairs"
    first = 0 if name == "sb_prompt" else 3
    q_spec = pl.BlockSpec((1, 1, tq, HEAD_DIM), lambda h, i: (first, h, i, 0))
    kv_spec = lambda n: pl.BlockSpec((1, 1, t, HEAD_DIM), lambda h, i: (n, h, 0, 0))
    g_spec = pl.BlockSpec((1, 1, HEAD_DIM), lambda h, i: (h, 0, 0))
    return pl.pallas_call(
        functools.partial(kernel_fn, tq=tq, tk=tk, n_chains=n_chains),
        grid=(N_HEADS, t // tq),
        in_specs=[q_spec, kv_spec(first + 1), kv_spec(first + 2)] + extra_specs(tq, tk) + [g_spec],
        out_specs=pl.BlockSpec((tq, HEAD_DIM), lambda h, i: (i, h)),
        out_shape=jax.ShapeDtypeStruct((t, D_GROUP), BF16),
        scratch_shapes=[pltpu.VMEM((2, n_chains, rc, tk), F32), pltpu.VMEM((2, n_chains, rc, tk), BF16),
                        pltpu.VMEM((n_chains, rc, acc_lanes), F32)],
        compiler_params=_params(("parallel", "arbitrary")),
        name=name,
    )


def _prompt_attention(qkv, c_rows, g_sb, g_fx):
    t = qkv.shape[2]
    o_sb = _attn_call(_sb_prompt_kernel, "sb_prompt", SB_TILES, t, HEAD_DIM, lambda tq, tk: [])(qkv, qkv, qkv, g_sb)
    fox_tk = min(FOX_TILES[1], t)
    c_specs = lambda tq, tk: [pl.BlockSpec((1, tq, 1), lambda h, i: (h, i, 0)),
                              pl.BlockSpec((1, t // tk, 1, tk), lambda h, i: (h, 0, 0, 0))]
    o_fx = _attn_call(_fox_prompt_kernel, "fox_prompt", FOX_TILES, t, 2 * HEAD_DIM, c_specs)(
        qkv, qkv, qkv, c_rows.reshape(N_HEADS, t, 1), c_rows.reshape(N_HEADS, t // fox_tk, 1, fox_tk), g_fx)
    return o_sb, o_fx


def _sb_sample_kernel(q_ref, k_ref, v_ref, pk_ref, pv_ref, g_ref, o_ref, *, blk):
    tq = q_ref.shape[3]
    past = pk_ref.shape[3]
    q = q_ref[0, 0, 0]
    states = [(jnp.zeros((tq, 1), F32), jnp.zeros((tq, HEAD_DIM), F32))]
    (carry, acc), = _sb_step([q], k_ref[0, 0, 0], v_ref[0, 0, 0], _suffix_ones(tq), states,
                             _diag_masks(0, tq, tq, 1, True))
    suffix = _suffix_ones(blk)
    blocks = range(past // blk)
    k_blk = lambda j: pk_ref[0, 0, 0, j * blk:(j + 1) * blk, :].astype(BF16)
    v_blk = lambda j: pv_ref[0, 0, 0, j * blk:(j + 1) * blk, :].astype(BF16)
    z = [_dot_nt(q, k_blk(j)) for j in blocks]
    incl = [_dot(_softplus2(z[j]).astype(BF16), suffix) for j in blocks]
    for j in reversed(blocks):
        acc = acc + _dot(jnp.exp2(z[j] - incl[j] - carry).astype(BF16), v_blk(j))
        carry = carry + incl[j][:, :1]
    o_ref[...] = _head_norm(acc, g_ref[0])


def _fox_sample_kernel(q_ref, k_ref, v_ref, pk_ref, pv_ref, cq_ref, cn_ref, cp_ref, g_ref, o_ref, *, blk):
    tq = q_ref.shape[3]
    past = pk_ref.shape[3]
    q = q_ref[0, 0, 0]
    cq = cq_ref[0, 0]
    blocks = range(past // blk)
    k_blk = lambda j: pk_ref[0, 0, 0, j * blk:(j + 1) * blk, :].astype(BF16)
    v_blk = lambda j: pv_ref[0, 0, 0, j * blk:(j + 1) * blk, :].astype(BF16)
    (mask,) = _diag_masks(0, tq, tq, 1, False)
    s_new = jnp.where(mask, _dot_nt(q, k_ref[0, 0, 0]) + (cq - cn_ref[0, 0]), -jnp.inf)
    s = [_dot_nt(q, k_blk(j)) + (cq - cp_ref[0, 0, j]) for j in blocks]
    m = jnp.max(s_new, axis=-1, keepdims=True)
    for j in blocks:
        m = jnp.maximum(m, jnp.max(s[j], axis=-1, keepdims=True))
    p_new = jnp.exp2(s_new - m)
    l = jnp.sum(p_new, axis=-1, keepdims=True)
    acc = _dot(p_new.astype(BF16), v_ref[0, 0, 0])
    for j in blocks:
        p = jnp.exp2(s[j] - m)
        l = l + jnp.sum(p, axis=-1, keepdims=True)
        acc = acc + _dot(p.astype(BF16), v_blk(j))
    o_ref[...] = _head_norm(acc / l, g_ref[0])


def _sample_attention(qkv, past_sb_k, past_sb_v, past_fx_k, past_fx_v, c_all, g_sb, g_fx, blk):
    b, _, past, _ = past_sb_k.shape
    tq = qkv.shape[2] // b
    qkv = qkv.reshape(6, N_HEADS, b, tq, HEAD_DIM)
    new_spec = lambda n: pl.BlockSpec((1, 1, 1, tq, HEAD_DIM), lambda bi, h: (n, h, bi, 0, 0))
    past_spec = pl.BlockSpec((1, 1, 1, past, HEAD_DIM), lambda bi, h: (0, bi, h, 0, 0))
    g_spec = pl.BlockSpec((1, 1, HEAD_DIM), lambda bi, h: (h, 0, 0))
    o_spec = pl.BlockSpec((tq, HEAD_DIM), lambda bi, h: (bi, h))
    o_shape = jax.ShapeDtypeStruct((b * tq, D_GROUP), BF16)
    o_sb = pl.pallas_call(
        functools.partial(_sb_sample_kernel, blk=blk),
        grid=(b, N_HEADS),
        in_specs=[new_spec(0), new_spec(1), new_spec(2), past_spec, past_spec, g_spec],
        out_specs=o_spec, out_shape=o_shape,
        compiler_params=_params(("parallel", "parallel")),
        name="sb_sample",
    )(qkv, qkv, qkv, past_sb_k[None], past_sb_v[None], g_sb)
    n_blk = past // blk
    cq = c_all[:, :, past:].reshape(b, N_HEADS, tq, 1)
    cn = c_all[:, :, past:].reshape(b, N_HEADS, 1, tq)
    cp = c_all[:, :, :past].reshape(b, N_HEADS, n_blk, 1, blk)
    o_fx = pl.pallas_call(
        functools.partial(_fox_sample_kernel, blk=blk),
        grid=(b, N_HEADS),
        in_specs=[new_spec(3), new_spec(4), new_spec(5), past_spec, past_spec,
                  pl.BlockSpec((1, 1, tq, 1), lambda bi, h: (bi, h, 0, 0)),
                  pl.BlockSpec((1, 1, 1, tq), lambda bi, h: (bi, h, 0, 0)),
                  pl.BlockSpec((1, 1, n_blk, 1, blk), lambda bi, h: (bi, h, 0, 0, 0)),
                  g_spec],
        out_specs=o_spec, out_shape=o_shape,
        compiler_params=_params(("parallel", "parallel")),
        name="fox_sample",
    )(qkv, qkv, qkv, past_fx_k[None], past_fx_v[None], cq, cn, cp, g_fx)
    return o_sb, o_fx


def _out_proj_kernel(x_ref, osb_ref, ofx_ref, wsb_ref, wfx_ref, o_ref):
    o_ref[...] = x_ref[...] + _dot(osb_ref[...], wsb_ref[...]) + _dot(ofx_ref[...], wfx_ref[...])


def _out_proj(x, o_sb, o_fx, w_out, tm):
    m, d = x.shape
    row = lambda i: (i, 0)
    return pl.pallas_call(
        _out_proj_kernel,
        grid=(m // tm,),
        in_specs=[pl.BlockSpec((tm, d), row), pl.BlockSpec((tm, D_GROUP), row), pl.BlockSpec((tm, D_GROUP), row),
                  pl.BlockSpec((D_GROUP, d), lambda i: (0, 0)), pl.BlockSpec((D_GROUP, d), lambda i: (1, 0))],
        out_specs=pl.BlockSpec((tm, d), row),
        out_shape=jax.ShapeDtypeStruct((m, d), F32),
        compiler_params=_params(("parallel",)),
        name="out_proj",
    )(x, o_sb, o_fx, w_out, w_out)


def _ple_kernel(x_ref, xn_ref, p_ref, wg_ref, wp_ref, gf_ref, o_ref):
    gate = jax.nn.sigmoid(_dot(xn_ref[...], wg_ref[...]))
    emb = _dot(p_ref[...].astype(BF16), wp_ref[...])
    o_ref[...] = _rms(x_ref[...] + gate * emb, gf_ref[...])


def _ple_final(x, xn, p, w_gate, w_proj, g_final, tm):
    m, d = x.shape
    ple = p.shape[1]
    row = lambda i: (i, 0)
    full = lambda i: (0, 0)
    return pl.pallas_call(
        _ple_kernel,
        grid=(m // tm,),
        in_specs=[pl.BlockSpec((tm, d), row), pl.BlockSpec((tm, d), row), pl.BlockSpec((tm, ple), row),
                  pl.BlockSpec((d, d), full), pl.BlockSpec((ple, d), full), pl.BlockSpec((1, d), full)],
        out_specs=pl.BlockSpec((tm, d), row),
        out_shape=jax.ShapeDtypeStruct((m, d), F32),
        compiler_params=_params(("parallel",)),
        name="ple_final",
    )(x, xn, p, w_gate, w_proj, g_final)


def _pad_lanes(x, n):
    return jnp.pad(x, ((0, 0), (0, n - x.shape[1])))


def _run_group(x, p, past, w, attn_blk):
    b, t, d = x.shape
    m = b * t
    tm = min(512, m)
    row = lambda g: g.reshape(1, -1)
    x0 = x.reshape(m, d)
    x1, xn = _ffn(x0, row(w["g_ffn1"]), w["ffn1_gate"], w["ffn1_up"], w["ffn1_down"], row(w["g_mix"]), tm, 512)
    qkv, k_sb, v_sb, k_fx, v_fx, lf_cols = _proj(xn, w["in_qkv"], w["in_f"], w["b_f"], tm)
    lf = lf_cols[:, :N_HEADS].reshape(b, t, N_HEADS).transpose(0, 2, 1)
    g_sb = w["g_out_sb"].reshape(N_HEADS, 1, HEAD_DIM)
    g_fx = w["g_out_fox"].reshape(N_HEADS, 1, HEAD_DIM)
    if past is None:
        c_rows = _cumsum_rows(lf.reshape(N_HEADS, t))
        o_sb, o_fx = _prompt_attention(qkv, c_rows, g_sb, g_fx)
    else:
        lf_all = jnp.concatenate([past[4], lf], axis=-1).reshape(b * N_HEADS, -1)
        total = lf_all.shape[1]
        padded = -(-total // LANES) * LANES
        c_all = _cumsum_rows(_pad_lanes(lf_all, padded))[:, :total].reshape(b, N_HEADS, total)
        o_sb, o_fx = _sample_attention(qkv, past[0], past[1], past[2], past[3], c_all, g_sb, g_fx, attn_blk)
    x2 = _out_proj(x1, o_sb, o_fx, w["out"], tm)
    x3, xn3 = _ffn(x2, row(w["g_ffn2"]), w["ffn2_gate"], w["ffn2_up"], w["ffn2_down"], row(w["g_ple"]), tm, 512)
    y = _ple_final(x3, xn3, p.reshape(m, -1), w["ple_gate"], w["ple_proj"], row(w["g_final"]), min(256, m))
    heads = lambda a: a.reshape(N_HEADS, b, t, HEAD_DIM).transpose(1, 0, 2, 3)[None]
    return y.reshape(b, t, d), heads(k_sb), heads(v_sb), heads(k_fx), heads(v_fx), lf[None]


def kernel(x_prompt, x_sample, cache_sb_k, cache_sb_v, cache_fox_k, cache_fox_v, cache_fox_logf, p_prompt, p_sample, g_ffn1, w_ffn1_gate, w_ffn1_up, w_ffn1_down, g_mix, w_in, b_forget, g_out_sb, g_out_fox, w_out, g_ffn2, w_ffn2_gate, w_ffn2_up, w_ffn2_down, g_ple, w_ple_gate, w_ple_proj, g_final):
    assert g_ffn1.shape[0] == 1, "single-layer trunk"
    bf = lambda a: a.astype(BF16)
    n_qkv = 6 * D_GROUP
    w = {
        "g_ffn1": g_ffn1[0], "ffn1_gate": bf(w_ffn1_gate[0]), "ffn1_up": bf(w_ffn1_up[0]), "ffn1_down": bf(w_ffn1_down[0]),
        "g_mix": g_mix[0], "in_qkv": bf(w_in[0]),
        "in_f": _pad_lanes(bf(w_in[0][:, n_qkv:]), LANES), "b_f": _pad_lanes(b_forget[0][None, :], LANES),
        "g_out_sb": g_out_sb[0], "g_out_fox": g_out_fox[0],
        "out": bf(w_out[0]),
        "g_ffn2": g_ffn2[0], "ffn2_gate": bf(w_ffn2_gate[0]), "ffn2_up": bf(w_ffn2_up[0]), "ffn2_down": bf(w_ffn2_down[0]),
        "g_ple": g_ple[0], "ple_gate": bf(w_ple_gate[0]), "ple_proj": bf(w_ple_proj[0]), "g_final": g_final,
    }
    past = (cache_sb_k[0], cache_sb_v[0], cache_fox_k[0], cache_fox_v[0], cache_fox_logf[0])
    y_p, sb_k_p, sb_v_p, fx_k_p, fx_v_p, lf_p = _run_group(x_prompt, p_prompt[0], None, w, 256)
    y_s, sb_k_s, sb_v_s, fx_k_s, fx_v_s, lf_s = _run_group(x_sample, p_sample[0], past, w, 256)
    return (y_p, y_s, sb_k_p, sb_v_p, fx_k_p, fx_v_p, lf_p, sb_k_s, sb_v_s, fx_k_s, fx_v_s, lf_s)
```

```python
import functools
import math

import jax
import jax.numpy as jnp
from jax import lax
from jax.experimental import pallas as pl
from jax.experimental.pallas import tpu as pltpu

HEAD_DIM = 128
N_HEADS = 8
D_GROUP = N_HEADS * HEAD_DIM
MACARON = 0.5
EPS = 1e-6
SCALE = 1.0 / math.sqrt(HEAD_DIM)
LOG2E = math.log2(math.e)
LANES = 128
V7X_VMEM_LIMIT = 56 * 1024 * 1024

F32 = jnp.float32
BF16 = jnp.bfloat16


def _params(semantics):
    return pltpu.CompilerParams(dimension_semantics=semantics, vmem_limit_bytes=V7X_VMEM_LIMIT)


def _rms(x, g):
    return x * lax.rsqrt(jnp.mean(x * x, axis=-1, keepdims=True) + EPS) * g


def _dot(a, b):
    return jnp.dot(a, b, preferred_element_type=F32)


def _dot_nt(a, b):
    return lax.dot_general(a, b, (((1,), (1,)), ((), ())), preferred_element_type=F32)


def _split3(x):
    hi = x.astype(BF16)
    r = x - hi.astype(F32)
    mid = r.astype(BF16)
    lo = (r - mid.astype(F32)).astype(BF16)
    return hi, mid, lo


def _softplus2(z2):
    return jnp.maximum(z2, 0.0) + jnp.log(1.0 + jnp.exp2(-jnp.abs(z2))) * LOG2E


def _ffn_kernel(x_ref, g_ref, wg_ref, wu_ref, wd_ref, gn_ref, o_ref, on_ref, xn_ref):
    f = pl.program_id(1)

    @pl.when(f == 0)
    def _():
        xn_ref[...] = _rms(x_ref[...], g_ref[...]).astype(BF16)
        o_ref[...] = jnp.zeros_like(o_ref)

    xn = xn_ref[...]
    gate = _dot(xn, wg_ref[...])
    up = _dot(xn, wu_ref[...])
    h = (gate * jax.nn.sigmoid(gate) * up).astype(BF16)
    o_ref[...] += _dot(h, wd_ref[...])

    @pl.when(f == pl.num_programs(1) - 1)
    def _():
        y = x_ref[...] + MACARON * o_ref[...]
        o_ref[...] = y
        on_ref[...] = _rms(y, gn_ref[...]).astype(BF16)


def _ffn(x, g, wg, wu, wd, g_next, tm, tf):
    m, d = x.shape
    d_ff = wg.shape[1]
    row = lambda i, f: (i, 0)
    return pl.pallas_call(
        _ffn_kernel,
        grid=(m // tm, d_ff // tf),
        in_specs=[
            pl.BlockSpec((tm, d), row),
            pl.BlockSpec((1, d), lambda i, f: (0, 0)),
            pl.BlockSpec((d, tf), lambda i, f: (0, f)),
            pl.BlockSpec((d, tf), lambda i, f: (0, f)),
            pl.BlockSpec((tf, d), lambda i, f: (f, 0)),
            pl.BlockSpec((1, d), lambda i, f: (0, 0)),
        ],
        out_specs=[pl.BlockSpec((tm, d), row), pl.BlockSpec((tm, d), row)],
        out_shape=[jax.ShapeDtypeStruct((m, d), F32), jax.ShapeDtypeStruct((m, d), BF16)],
        scratch_shapes=[pltpu.VMEM((tm, d), BF16)],
        compiler_params=_params(("parallel", "arbitrary")),
        name="ffn",
    )(x, g, wg, wu, wd, g_next)


def _proj_kernel(xn_ref, w_ref, wf_ref, bf_ref, qkv_ref, ksb_ref, vsb_ref, kfx_ref, vfx_ref, lf_ref):
    n = pl.program_id(1)
    xn = xn_ref[...]
    res = _dot(xn, w_ref[...])
    is_q = jnp.logical_or(n == 0, n == 3)
    res_mx = res * jnp.where(is_q, SCALE * LOG2E, 1.0)
    for h in range(N_HEADS):
        qkv_ref[0, h] = res_mx[:, h * HEAD_DIM:(h + 1) * HEAD_DIM].astype(BF16)

    def store_heads(ref):
        for h in range(N_HEADS):
            ref[h] = res[:, h * HEAD_DIM:(h + 1) * HEAD_DIM]

    for idx, ref in ((1, ksb_ref), (2, vsb_ref), (4, kfx_ref), (5, vfx_ref)):
        pl.when(n == idx)(functools.partial(store_heads, ref))

    @pl.when(n == 0)
    def _():
        fz = _dot(xn, wf_ref[...]) + bf_ref[...]
        lf_ref[...] = jnp.minimum(fz, 0.0) - jnp.log1p(jnp.exp(-jnp.abs(fz)))


def _proj(xn, w_qkv, w_f, b_f, tm):
    m, d = xn.shape
    head_f32 = jax.ShapeDtypeStruct((N_HEADS, m, HEAD_DIM), F32)
    head_spec = pl.BlockSpec((N_HEADS, tm, HEAD_DIM), lambda i, n: (0, i, 0))
    return pl.pallas_call(
        _proj_kernel,
        grid=(m // tm, 6),
        in_specs=[
            pl.BlockSpec((tm, d), lambda i, n: (i, 0)),
            pl.BlockSpec((d, D_GROUP), lambda i, n: (0, n)),
            pl.BlockSpec((d, LANES), lambda i, n: (0, 0)),
            pl.BlockSpec((1, LANES), lambda i, n: (0, 0)),
        ],
        out_specs=[
            pl.BlockSpec((1, N_HEADS, tm, HEAD_DIM), lambda i, n: (n, 0, i, 0)),
            head_spec, head_spec, head_spec, head_spec,
            pl.BlockSpec((tm, LANES), lambda i, n: (i, 0)),
        ],
        out_shape=[
            jax.ShapeDtypeStruct((6, N_HEADS, m, HEAD_DIM), BF16),
            head_f32, head_f32, head_f32, head_f32,
            jax.ShapeDtypeStruct((m, LANES), F32),
        ],
        compiler_params=_params(("parallel", "arbitrary")),
        name="in_proj",
    )(xn, w_qkv, w_f, b_f)


def _cumsum_kernel(x_ref, o_ref):
    n_chunks, rows, _ = x_ref.shape
    upper = (lax.broadcasted_iota(jnp.int32, (LANES, LANES), 0)
             <= lax.broadcasted_iota(jnp.int32, (LANES, LANES), 1)).astype(BF16)

    def body(c, carry):
        hi, mid, lo = _split3(x_ref[c])
        cs = _dot(hi, upper) + _dot(mid, upper) + _dot(lo, upper) + carry
        o_ref[c] = cs * LOG2E
        return cs[:, LANES - 1:LANES]

    lax.fori_loop(0, n_chunks, body, jnp.zeros((rows, 1), F32))


def _cumsum_rows(x):
    rows, length = x.shape
    xc = x.reshape(rows, length // LANES, LANES).transpose(1, 0, 2)
    out = pl.pallas_call(
        _cumsum_kernel,
        out_shape=jax.ShapeDtypeStruct(xc.shape, F32),
        compiler_params=pltpu.CompilerParams(vmem_limit_bytes=V7X_VMEM_LIMIT),
        name="cumsum",
    )(xc)
    return out.transpose(1, 0, 2).reshape(rows, length)


def _suffix_ones(n):
    return (lax.broadcasted_iota(jnp.int32, (n, n), 0)
            >= lax.broadcasted_iota(jnp.int32, (n, n), 1)).astype(BF16)


def _sb_step(qs, k, v, suffix, states, masks):
    live = [c for c, m in enumerate(masks) if not isinstance(m, str)]
    z = {c: _dot_nt(qs[c], k) for c in live}
    sp = {}
    for c in live:
        s = _softplus2(z[c])
        sp[c] = s if masks[c] is None else jnp.where(masks[c], s, 0.0)
    incl = {c: _dot(sp[c].astype(BF16), suffix) for c in live}
    out = list(states)
    for c in live:
        carry, acc = states[c]
        a = jnp.exp2(z[c] - incl[c] - carry)
        if masks[c] is not None:
            a = jnp.where(masks[c], a, 0.0)
        out[c] = (carry + incl[c][:, :1], acc + _dot(a.astype(BF16), v))
    return out


def _fox_step(qs, k, v, biases, states, masks):
    live = [c for c, m in enumerate(masks) if not isinstance(m, str)]
    s = {}
    for c in live:
        sc = _dot_nt(qs[c], k) + biases[c]
        s[c] = sc if masks[c] is None else jnp.where(masks[c], sc, -jnp.inf)
    out = list(states)
    for c in live:
        m, l, acc = states[c]
        m_new = jnp.maximum(m, jnp.max(s[c], axis=-1, keepdims=True))
        alpha = jnp.exp2(m - m_new)
        p = jnp.exp2(s[c] - m_new)
        out[c] = (m_new, alpha * l + jnp.sum(p, axis=-1, keepdims=True), alpha * acc + _dot(p.astype(BF16), v))
    return out


def _diag_masks(jj, tk, rc, n_chains, strict):
    masks = []
    for c in range(n_chains):
        lo_row, hi_row = c * rc, (c + 1) * rc - 1
        lo_key, hi_key = jj * tk, (jj + 1) * tk - 1
        if (lo_key >= hi_row) if strict else (lo_key > hi_row):
            masks.append("skip")
        elif (hi_key < lo_row) if strict else (hi_key <= lo_row):
            masks.append(None)
        else:
            rows = lax.broadcasted_iota(jnp.int32, (rc, tk), 0) + lo_row
            cols = lax.broadcasted_iota(jnp.int32, (rc, tk), 1) + lo_key
            masks.append(cols < rows if strict else cols <= rows)
    return masks


def _head_norm(o, g):
    return (o * lax.rsqrt(jnp.mean(o * o, axis=-1, keepdims=True) + EPS) * g).astype(BF16)


def _sb_prompt_kernel(q_ref, k_ref, v_ref, g_ref, o_ref, z_ref, a_ref, acc_ref, *, tq, tk, n_chains):
    qi = pl.program_id(1)
    rc = tq // n_chains
    ratio = tq // tk
    qs = [q_ref[0, 0, c * rc:(c + 1) * rc, :] for c in range(n_chains)]
    suffix = _suffix_ones(tk)

    def kv(j):
        start = pl.multiple_of(j * tk, tk)
        return k_ref[0, 0, pl.ds(start, tk), :], v_ref[0, 0, pl.ds(start, tk), :]

    states = [(jnp.zeros((rc, 1), F32), jnp.zeros((rc, HEAD_DIM), F32)) for _ in range(n_chains)]
    for jj in reversed(range(ratio)):
        k, v = kv(qi * ratio + jj)
        states = _sb_step(qs, k, v, suffix, states, _diag_masks(jj, tk, rc, n_chains, True))

    n_main = qi * ratio
    chains = range(n_chains)

    def put_scores(slot, j):
        k, _ = kv(jnp.maximum(j, 0))
        for c in chains:
            z_ref[slot, c] = _dot_nt(qs[c], k)

    def half_trip(src, dst, j, j_prev, carries):
        put_scores(dst, j - 1)
        _, v_prev = kv(j_prev)
        for c in chains:
            acc_ref[c] += _dot(a_ref[src, c], v_prev)
        z = [z_ref[src, c] for c in chains]
        incl = [_dot(_softplus2(z[c]).astype(BF16), suffix) for c in chains]
        for c in chains:
            a_ref[dst, c] = jnp.exp2(z[c] - incl[c] - carries[c]).astype(BF16)
        return [carries[c] + incl[c][:, :1] for c in chains]

    def body(pair, loop_state):
        j_prev, carries = loop_state
        j = n_main - 1 - 2 * pair
        carries = half_trip(0, 1, j, j_prev, carries)
        carries = half_trip(1, 0, j - 1, j, carries)
        return j - 1, carries

    put_scores(0, n_main - 1)
    for c in chains:
        a_ref[0, c] = jnp.zeros((rc, tk), BF16)
        acc_ref[c] = states[c][1]
    j_last, _ = lax.fori_loop(0, n_main // 2, body, (0, [states[c][0] for c in chains]))
    _, v_last = kv(j_last)
    for c in chains:
        o_ref[c * rc:(c + 1) * rc, :] = _head_norm(acc_ref[c] + _dot(a_ref[0, c], v_last), g_ref[0])


def _fox_prompt_kernel(q_ref, k_ref, v_ref, cq_ref, ck_ref, g_ref, o_ref, z_ref, p_ref, acc_ref, *, tq, tk, n_chains):
    qi = pl.program_id(1)
    rc = tq // n_chains
    ratio = tq // tk
    qs = [q_ref[0, 0, c * rc:(c + 1) * rc, :] for c in range(n_chains)]
    cqs = [cq_ref[0, c * rc:(c + 1) * rc, :] for c in range(n_chains)]

    def kv(j):
        start = pl.multiple_of(j * tk, tk)
        return k_ref[0, 0, pl.ds(start, tk), :], v_ref[0, 0, pl.ds(start, tk), :]

    states = [(jnp.full((rc, 1), -jnp.inf, F32), jnp.zeros((rc, 1), F32), jnp.zeros((rc, HEAD_DIM), F32))
              for _ in range(n_chains)]
    for jj in range(ratio):
        k, v = kv(qi * ratio + jj)
        ck = ck_ref[0, qi * ratio + jj]
        states = _fox_step(qs, k, v, [cq - ck for cq in cqs], states, _diag_masks(jj, tk, rc, n_chains, False))

    n_main = qi * ratio
    chains = range(n_chains)

    def put_scores(slot, j):
        k, _ = kv(jnp.maximum(j, 0))
        for c in chains:
            z_ref[slot, c] = _dot_nt(qs[c], k)

    lane = lax.broadcasted_iota(jnp.int32, (1, HEAD_DIM), 1)
    ones_col = jnp.broadcast_to(jnp.where(lane == 0, 1.0, 0.0).astype(BF16), (tk, HEAD_DIM))

    def values(j):
        _, v = kv(j)
        return jnp.concatenate([v, ones_col], axis=1)

    def half_trip(src, dst, j, j_prev, maxes):
        put_scores(dst, j - 1)
        v_prev = values(j_prev)
        ck = ck_ref[0, j]
        out = []
        for c in chains:
            s = z_ref[src, c] + (cqs[c] - ck)
            m_new = jnp.maximum(maxes[c], jnp.max(s, axis=-1, keepdims=True))
            acc_ref[c] = (acc_ref[c] + _dot(p_ref[src, c], v_prev)) * jnp.exp2(maxes[c] - m_new)
            p_ref[dst, c] = jnp.exp2(s - m_new).astype(BF16)
            out.append(m_new)
        return out

    def body(pair, loop_state):
        j_prev, maxes = loop_state
        j = n_main - 1 - 2 * pair
        maxes = half_trip(0, 1, j, j_prev, maxes)
        maxes = half_trip(1, 0, j - 1, j, maxes)
        return j - 1, maxes

    put_scores(0, n_main - 1)
    for c in chains:
        p_ref[0, c] = jnp.zeros((rc, tk), BF16)
        acc_ref[c] = jnp.concatenate([states[c][2], jnp.where(lane == 0, states[c][1], 0.0)], axis=1)
    j_last, _ = lax.fori_loop(0, n_main // 2, body, (0, [states[c][0] for c in chains]))
    v_last = values(j_last)
    for c in chains:
        acc = acc_ref[c] + _dot(p_ref[0, c], v_last)
        o_ref[c * rc:(c + 1) * rc, :] = _head_norm(acc[:, :HEAD_DIM] / acc[:, HEAD_DIM:HEAD_DIM + 1], g_ref[0])


SB_TILES = (1024, 256, 4)
FOX_TILES = (1024, 512, 2)


def _attn_call(kernel_fn, name, tiles, t, acc_lanes, extra_specs):
    tq, tk, n_chains = (min(tiles[0], t), min(tiles[1], t), tiles[2])
    rc = tq // n_chains
    assert (tq // tk) % 2 == 0, "the pipelined sweep walks key blocks in pairs"
    first = 0 if name == "sb_prompt" else 3
    q_spec = pl.BlockSpec((1, 1, tq, HEAD_DIM), lambda h, i: (first, h, i, 0))
    kv_spec = lambda n: pl.BlockSpec((1, 1, t, HEAD_DIM), lambda h, i: (n, h, 0, 0))
    g_spec = pl.BlockSpec((1, 1, HEAD_DIM), lambda h, i: (h, 0, 0))
    return pl.pallas_call(
        functools.partial(kernel_fn, tq=tq, tk=tk, n_chains=n_chains),
        grid=(N_HEADS, t // tq),
        in_specs=[q_spec, kv_spec(first + 1), kv_spec(first + 2)] + extra_specs(tq, tk) + [g_spec],
        out_specs=pl.BlockSpec((tq, HEAD_DIM), lambda h, i: (i, h)),
        out_shape=jax.ShapeDtypeStruct((t, D_GROUP), BF16),
        scratch_shapes=[pltpu.VMEM((2, n_chains, rc, tk), F32), pltpu.VMEM((2, n_chains, rc, tk), BF16),
                        pltpu.VMEM((n_chains, rc, acc_lanes), F32)],
        compiler_params=_params(("parallel", "arbitrary")),
        name=name,
    )


def _prompt_attention(qkv, c_rows, g_sb, g_fx):
    t = qkv.shape[2]
    o_sb = _attn_call(_sb_prompt_kernel, "sb_prompt", SB_TILES, t, HEAD_DIM, lambda tq, tk: [])(qkv, qkv, qkv, g_sb)
    fox_tk = min(FOX_TILES[1], t)
    c_specs = lambda tq, tk: [pl.BlockSpec((1, tq, 1), lambda h, i: (h, i, 0)),
                              pl.BlockSpec((1, t // tk, 1, tk), lambda h, i: (h, 0, 0, 0))]
    o_fx = _attn_call(_fox_prompt_kernel, "fox_prompt", FOX_TILES, t, 2 * HEAD_DIM, c_specs)(
        qkv, qkv, qkv, c_rows.reshape(N_HEADS, t, 1), c_rows.reshape(N_HEADS, t // fox_tk, 1, fox_tk), g_fx)
    return o_sb, o_fx


def _sb_sample_kernel(q_ref, k_ref, v_ref, pk_ref, pv_ref, g_ref, o_ref, *, blk):
    tq = q_ref.shape[3]
    past = pk_ref.shape[3]
    q = q_ref[0, 0, 0]
    states = [(jnp.zeros((tq, 1), F32), jnp.zeros((tq, HEAD_DIM), F32))]
    (carry, acc), = _sb_step([q], k_ref[0, 0, 0], v_ref[0, 0, 0], _suffix_ones(tq), states,
                             _diag_masks(0, tq, tq, 1, True))
    suffix = _suffix_ones(blk)
    blocks = range(past // blk)
    k_blk = lambda j: pk_ref[0, 0, 0, j * blk:(j + 1) * blk, :].astype(BF16)
    v_blk = lambda j: pv_ref[0, 0, 0, j * blk:(j + 1) * blk, :].astype(BF16)
    z = [_dot_nt(q, k_blk(j)) for j in blocks]
    incl = [_dot(_softplus2(z[j]).astype(BF16), suffix) for j in blocks]
    for j in reversed(blocks):
        acc = acc + _dot(jnp.exp2(z[j] - incl[j] - carry).astype(BF16), v_blk(j))
        carry = carry + incl[j][:, :1]
    o_ref[...] = _head_norm(acc, g_ref[0])


def _fox_sample_kernel(q_ref, k_ref, v_ref, pk_ref, pv_ref, cq_ref, cn_ref, cp_ref, g_ref, o_ref, *, blk):
    tq = q_ref.shape[3]
    past = pk_ref.shape[3]
    q = q_ref[0, 0, 0]
    cq = cq_ref[0, 0]
    blocks = range(past // blk)
    k_blk = lambda j: pk_ref[0, 0, 0, j * blk:(j + 1) * blk, :].astype(BF16)
    v_blk = lambda j: pv_ref[0, 0, 0, j * blk:(j + 1) * blk, :].astype(BF16)
    (mask,) = _diag_masks(0, tq, tq, 1, False)
    s_new = jnp.where(mask, _dot_nt(q, k_ref[0, 0, 0]) + (cq - cn_ref[0, 0]), -jnp.inf)
    s = [_dot_nt(q, k_blk(j)) + (cq - cp_ref[0, 0, j]) for j in blocks]
    m = jnp.max(s_new, axis=-1, keepdims=True)
    for j in blocks:
        m = jnp.maximum(m, jnp.max(s[j], axis=-1, keepdims=True))
    p_new = jnp.exp2(s_new - m)
    l = jnp.sum(p_new, axis=-1, keepdims=True)
    acc = _dot(p_new.astype(BF16), v_ref[0, 0, 0])
    for j in blocks:
        p = jnp.exp2(s[j] - m)
        l = l + jnp.sum(p, axis=-1, keepdims=True)
        acc = acc + _dot(p.astype(BF16), v_blk(j))
    o_ref[...] = _head_norm(acc / l, g_ref[0])


def _sample_attention(qkv, past_sb_k, past_sb_v, past_fx_k, past_fx_v, c_all, g_sb, g_fx, blk):
    b, _, past, _ = past_sb_k.shape
    tq = qkv.shape[2] // b
    qkv = qkv.reshape(6, N_HEADS, b, tq, HEAD_DIM)
    new_spec = lambda n: pl.BlockSpec((1, 1, 1, tq, HEAD_DIM), lambda bi, h: (n, h, bi, 0, 0))
    past_spec = pl.BlockSpec((1, 1, 1, past, HEAD_DIM), lambda bi, h: (0, bi, h, 0, 0))
    g_spec = pl.BlockSpec((1, 1, HEAD_DIM), lambda bi, h: (h, 0, 0))
    o_spec = pl.BlockSpec((tq, HEAD_DIM), lambda bi, h: (bi, h))
    o_shape = jax.ShapeDtypeStruct((b * tq, D_GROUP), BF16)
    o_sb = pl.pallas_call(
        functools.partial(_sb_sample_kernel, blk=blk),
        grid=(b, N_HEADS),
        in_specs=[new_spec(0), new_spec(1), new_spec(2), past_spec, past_spec, g_spec],
        out_specs=o_spec, out_shape=o_shape,
        compiler_params=_params(("parallel", "parallel")),
        name="sb_sample",
    )(qkv, qkv, qkv, past_sb_k[None], past_sb_v[None], g_sb)
    n_blk = past // blk
    cq = c_all[:, :, past:].reshape(b, N_HEADS, tq, 1)
    cn = c_all[:, :, past:].reshape(b, N_HEADS, 1, tq)
    cp = c_all[:, :, :past].reshape(b, N_HEADS, n_blk, 1, blk)
    o_fx = pl.pallas_call(
        functools.partial(_fox_sample_kernel, blk=blk),
        grid=(b, N_HEADS),
        in_specs=[new_spec(3), new_spec(4), new_spec(5), past_spec, past_spec,
                  pl.BlockSpec((1, 1, tq, 1), lambda bi, h: (bi, h, 0, 0)),
                  pl.BlockSpec((1, 1, 1, tq), lambda bi, h: (bi, h, 0, 0)),
                  pl.BlockSpec((1, 1, n_blk, 1, blk), lambda bi, h: (bi, h, 0, 0, 0)),
                  g_spec],
        out_specs=o_spec, out_shape=o_shape,
        compiler_params=_params(("parallel", "parallel")),
        name="fox_sample",
    )(qkv, qkv, qkv, past_fx_k[None], past_fx_v[None], cq, cn, cp, g_fx)
    return o_sb, o_fx


def _out_proj_kernel(x_ref, osb_ref, ofx_ref, wsb_ref, wfx_ref, o_ref):
    o_ref[...] = x_ref[...] + _dot(osb_ref[...], wsb_ref[...]) + _dot(ofx_ref[...], wfx_ref[...])


def _out_proj(x, o_sb, o_fx, w_out, tm):
    m, d = x.shape
    row = lambda i: (i, 0)
    return pl.pallas_call(
        _out_proj_kernel,
        grid=(m // tm,),
        in_specs=[pl.BlockSpec((tm, d), row), pl.BlockSpec((tm, D_GROUP), row), pl.BlockSpec((tm, D_GROUP), row),
                  pl.BlockSpec((D_GROUP, d), lambda i: (0, 0)), pl.BlockSpec((D_GROUP, d), lambda i: (1, 0))],
        out_specs=pl.BlockSpec((tm, d), row),
        out_shape=jax.ShapeDtypeStruct((m, d), F32),
        compiler_params=_params(("parallel",)),
        name="out_proj",
    )(x, o_sb, o_fx, w_out, w_out)


def _ple_kernel(x_ref, xn_ref, p_ref, wg_ref, wp_ref, gf_ref, o_ref):
    gate = jax.nn.sigmoid(_dot(xn_ref[...], wg_ref[...]))
    emb = _dot(p_ref[...].astype(BF16), wp_ref[...])
    o_ref[...] = _rms(x_ref[...] + gate * emb, gf_ref[...])


def _ple_final(x, xn, p, w_gate, w_proj, g_final, tm):
    m, d = x.shape
    ple = p.shape[1]
    row = lambda i: (i, 0)
    full = lambda i: (0, 0)
    return pl.pallas_call(
        _ple_kernel,
        grid=(m // tm,),
        in_specs=[pl.BlockSpec((tm, d), row), pl.BlockSpec((tm, d), row), pl.BlockSpec((tm, ple), row),
                  pl.BlockSpec((d, d), full), pl.BlockSpec((ple, d), full), pl.BlockSpec((1, d), full)],
        out_specs=pl.BlockSpec((tm, d), row),
        out_shape=jax.ShapeDtypeStruct((m, d), F32),
        compiler_params=_params(("parallel",)),
        name="ple_final",
    )(x, xn, p, w_gate, w_proj, g_final)


def _pad_lanes(x, n):
    return jnp.pad(x, ((0, 0), (0, n - x.shape[1])))


def _run_group(x, p, past, w, attn_blk):
    b, t, d = x.shape
    m = b * t
    tm = min(512, m)
    row = lambda g: g.reshape(1, -1)
    x0 = x.reshape(m, d)
    x1, xn = _ffn(x0, row(w["g_ffn1"]), w["ffn1_gate"], w["ffn1_up"], w["ffn1_down"], row(w["g_mix"]), tm, 512)
    qkv, k_sb, v_sb, k_fx, v_fx, lf_cols = _proj(xn, w["in_qkv"], w["in_f"], w["b_f"], tm)
    lf = lf_cols[:, :N_HEADS].reshape(b, t, N_HEADS).transpose(0, 2, 1)
    g_sb = w["g_out_sb"].reshape(N_HEADS, 1, HEAD_DIM)
    g_fx = w["g_out_fox"].reshape(N_HEADS, 1, HEAD_DIM)
    if past is None:
        c_rows = _cumsum_rows(lf.reshape(N_HEADS, t))
        o_sb, o_fx = _prompt_attention(qkv, c_rows, g_sb, g_fx)
    else:
        lf_all = jnp.concatenate([past[4], lf], axis=-1).reshape(b * N_HEADS, -1)
        total = lf_all.shape[1]
        padded = -(-total // LANES) * LANES
        c_all = _cumsum_rows(_pad_lanes(lf_all, padded))[:, :total].reshape(b, N_HEADS, total)
        o_sb, o_fx = _sample_attention(qkv, past[0], past[1], past[2], past[3], c_all, g_sb, g_fx, attn_blk)
    x2 = _out_proj(x1, o_sb, o_fx, w["out"], tm)
    x3, xn3 = _ffn(x2, row(w["g_ffn2"]), w["ffn2_gate"], w["ffn2_up"], w["ffn2_down"], row(w["g_ple"]), tm, 512)
    y = _ple_final(x3, xn3, p.reshape(m, -1), w["ple_gate"], w["ple_proj"], row(w["g_final"]), min(256, m))
    heads = lambda a: a.reshape(N_HEADS, b, t, HEAD_DIM).transpose(1, 0, 2, 3)[None]
    return y.reshape(b, t, d), heads(k_sb), heads(v_sb), heads(k_fx), heads(v_fx), lf[None]


def kernel(x_prompt, x_sample, cache_sb_k, cache_sb_v, cache_fox_k, cache_fox_v, cache_fox_logf, p_prompt, p_sample, g_ffn1, w_ffn1_gate, w_ffn1_up, w_ffn1_down, g_mix, w_in, b_forget, g_out_sb, g_out_fox, w_out, g_ffn2, w_ffn2_gate, w_ffn2_up, w_ffn2_down, g_ple, w_ple_gate, w_ple_proj, g_final):
    assert g_ffn1.shape[0] == 1, "single-layer trunk"
    bf = lambda a: a.astype(BF16)
    n_qkv = 6 * D_GROUP
    w = {
        "g_ffn1": g_ffn1[0], "ffn1_gate": bf(w_ffn1_gate[0]), "ffn1_up": bf(w_ffn1_up[0]), "ffn1_down": bf(w_ffn1_down[0]),
        "g_mix": g_mix[0], "in_qkv": bf(w_in[0]),
        "in_f": _pad_lanes(bf(w_in[0][:, n_qkv:]), LANES), "b_f": _pad_lanes(b_forget[0][None, :], LANES),
        "g_out_sb": g_out_sb[0], "g_out_fox": g_out_fox[0],
        "out": bf(w_out[0]),
        "g_ffn2": g_ffn2[0], "ffn2_gate": bf(w_ffn2_gate[0]), "ffn2_up": bf(w_ffn2_up[0]), "ffn2_down": bf(w_ffn2_down[0]),
        "g_ple": g_ple[0], "ple_gate": bf(w_ple_gate[0]), "ple_proj": bf(w_ple_proj[0]), "g_final": g_final,
    }
    past = (cache_sb_k[0], cache_sb_v[0], cache_fox_k[0], cache_fox_v[0], cache_fox_logf[0])
    y_p, sb_k_p, sb_v_p, fx_k_p, fx_v_p, lf_p = _run_group(x_prompt, p_prompt[0], None, w, 256)
    y_s, sb_k_s, sb_v_s, fx_k_s, fx_v_s, lf_s = _run_group(x_sample, p_sample[0], past, w, 256)
    return (y_p, y_s, sb_k_p, sb_v_p, fx_k_p, fx_v_p, lf_p, sb_k_s, sb_v_s, fx_k_s, fx_v_s, lf_s)
```

```python
import functools
import math

import jax
import jax.numpy as jnp
from jax import lax
from jax.experimental import pallas as pl
from jax.experimental.pallas import tpu as pltpu

HEAD_DIM = 128
N_HEADS = 8
D_GROUP = N_HEADS * HEAD_DIM
MACARON = 0.5
EPS = 1e-6
SCALE = 1.0 / math.sqrt(HEAD_DIM)
LOG2E = math.log2(math.e)
LANES = 128
V7X_VMEM_LIMIT = 56 * 1024 * 1024

F32 = jnp.float32
BF16 = jnp.bfloat16


def _params(semantics):
    return pltpu.CompilerParams(dimension_semantics=semantics, vmem_limit_bytes=V7X_VMEM_LIMIT)


def _rms(x, g):
    return x * lax.rsqrt(jnp.mean(x * x, axis=-1, keepdims=True) + EPS) * g


def _dot(a, b):
    return jnp.dot(a, b, preferred_element_type=F32)


def _dot_nt(a, b):
    return lax.dot_general(a, b, (((1,), (1,)), ((), ())), preferred_element_type=F32)


def _split3(x):
    hi = x.astype(BF16)
    r = x - hi.astype(F32)
    mid = r.astype(BF16)
    lo = (r - mid.astype(F32)).astype(BF16)
    return hi, mid, lo


def _softplus2(z2):
    return jnp.maximum(z2, 0.0) + jnp.log(1.0 + jnp.exp2(-jnp.abs(z2))) * LOG2E


def _ffn_kernel(x_ref, g_ref, wg_ref, wu_ref, wd_ref, gn_ref, o_ref, on_ref, xn_ref):
    f = pl.program_id(1)

    @pl.when(f == 0)
    def _():
        xn_ref[...] = _rms(x_ref[...], g_ref[...]).astype(BF16)
        o_ref[...] = jnp.zeros_like(o_ref)

    xn = xn_ref[...]
    gate = _dot(xn, wg_ref[...])
    up = _dot(xn, wu_ref[...])
    h = (gate * jax.nn.sigmoid(gate) * up).astype(BF16)
    o_ref[...] += _dot(h, wd_ref[...])

    @pl.when(f == pl.num_programs(1) - 1)
    def _():
        y = x_ref[...] + MACARON * o_ref[...]
        o_ref[...] = y
        on_ref[...] = _rms(y, gn_ref[...]).astype(BF16)


def _ffn(x, g, wg, wu, wd, g_next, tm, tf):
    m, d = x.shape
    d_ff = wg.shape[1]
    row = lambda i, f: (i, 0)
    return pl.pallas_call(
        _ffn_kernel,
        grid=(m // tm, d_ff // tf),
        in_specs=[
            pl.BlockSpec((tm, d), row),
            pl.BlockSpec((1, d), lambda i, f: (0, 0)),
            pl.BlockSpec((d, tf), lambda i, f: (0, f)),
            pl.BlockSpec((d, tf), lambda i, f: (0, f)),
            pl.BlockSpec((tf, d), lambda i, f: (f, 0)),
            pl.BlockSpec((1, d), lambda i, f: (0, 0)),
        ],
        out_specs=[pl.BlockSpec((tm, d), row), pl.BlockSpec((tm, d), row)],
        out_shape=[jax.ShapeDtypeStruct((m, d), F32), jax.ShapeDtypeStruct((m, d), BF16)],
        scratch_shapes=[pltpu.VMEM((tm, d), BF16)],
        compiler_params=_params(("parallel", "arbitrary")),
        name="ffn",
    )(x, g, wg, wu, wd, g_next)


def _proj_kernel(xn_ref, w_ref, wf_ref, bf_ref, qkv_ref, ksb_ref, vsb_ref, kfx_ref, vfx_ref, lf_ref):
    n = pl.program_id(1)
    xn = xn_ref[...]
    res = _dot(xn, w_ref[...])
    is_q = jnp.logical_or(n == 0, n == 3)
    res_mx = res * jnp.where(is_q, SCALE * LOG2E, 1.0)
    for h in range(N_HEADS):
        qkv_ref[0, h] = res_mx[:, h * HEAD_DIM:(h + 1) * HEAD_DIM].astype(BF16)

    def store_heads(ref):
        for h in range(N_HEADS):
            ref[h] = res[:, h * HEAD_DIM:(h + 1) * HEAD_DIM]

    for idx, ref in ((1, ksb_ref), (2, vsb_ref), (4, kfx_ref), (5, vfx_ref)):
        pl.when(n == idx)(functools.partial(store_heads, ref))

    @pl.when(n == 0)
    def _():
        fz = _dot(xn, wf_ref[...]) + bf_ref[...]
        lf_ref[...] = jnp.minimum(fz, 0.0) - jnp.log1p(jnp.exp(-jnp.abs(fz)))


def _proj(xn, w_qkv, w_f, b_f, tm):
    m, d = xn.shape
    head_f32 = jax.ShapeDtypeStruct((N_HEADS, m, HEAD_DIM), F32)
    head_spec = pl.BlockSpec((N_HEADS, tm, HEAD_DIM), lambda i, n: (0, i, 0))
    return pl.pallas_call(
        _proj_kernel,
        grid=(m // tm, 6),
        in_specs=[
            pl.BlockSpec((tm, d), lambda i, n: (i, 0)),
            pl.BlockSpec((d, D_GROUP), lambda i, n: (0, n)),
            pl.BlockSpec((d, LANES), lambda i, n: (0, 0)),
            pl.BlockSpec((1, LANES), lambda i, n: (0, 0)),
        ],
        out_specs=[
            pl.BlockSpec((1, N_HEADS, tm, HEAD_DIM), lambda i, n: (n, 0, i, 0)),
            head_spec, head_spec, head_spec, head_spec,
            pl.BlockSpec((tm, LANES), lambda i, n: (i, 0)),
        ],
        out_shape=[
            jax.ShapeDtypeStruct((6, N_HEADS, m, HEAD_DIM), BF16),
            head_f32, head_f32, head_f32, head_f32,
            jax.ShapeDtypeStruct((m, LANES), F32),
        ],
        compiler_params=_params(("parallel", "arbitrary")),
        name="in_proj",
    )(xn, w_qkv, w_f, b_f)


def _cumsum_kernel(x_ref, o_ref):
    n_chunks, rows, _ = x_ref.shape
    upper = (lax.broadcasted_iota(jnp.int32, (LANES, LANES), 0)
             <= lax.broadcasted_iota(jnp.int32, (LANES, LANES), 1)).astype(BF16)

    def body(c, carry):
        hi, mid, lo = _split3(x_ref[c])
        cs = _dot(hi, upper) + _dot(mid, upper) + _dot(lo, upper) + carry
        o_ref[c] = cs * LOG2E
        return cs[:, LANES - 1:LANES]

    lax.fori_loop(0, n_chunks, body, jnp.zeros((rows, 1), F32))


def _cumsum_rows(x):
    rows, length = x.shape
    xc = x.reshape(rows, length // LANES, LANES).transpose(1, 0, 2)
    out = pl.pallas_call(
        _cumsum_kernel,
        out_shape=jax.ShapeDtypeStruct(xc.shape, F32),
        compiler_params=pltpu.CompilerParams(vmem_limit_bytes=V7X_VMEM_LIMIT),
        name="cumsum",
    )(xc)
    return out.transpose(1, 0, 2).reshape(rows, length)


def _suffix_ones(n):
    return (lax.broadcasted_iota(jnp.int32, (n, n), 0)
            >= lax.broadcasted_iota(jnp.int32, (n, n), 1)).astype(BF16)


def _sb_step(qs, k, v, suffix, states, masks):
    live = [c for c, m in enumerate(masks) if not isinstance(m, str)]
    z = {c: _dot_nt(qs[c], k) for c in live}
    sp = {}
    for c in live:
        s = _softplus2(z[c])
        sp[c] = s if masks[c] is None else jnp.where(masks[c], s, 0.0)
    incl = {c: _dot(sp[c].astype(BF16), suffix) for c in live}
    out = list(states)
    for c in live:
        carry, acc = states[c]
        a = jnp.exp2(z[c] - incl[c] - carry)
        if masks[c] is not None:
            a = jnp.where(masks[c], a, 0.0)
        out[c] = (carry + incl[c][:, :1], acc + _dot(a.astype(BF16), v))
    return out


def _fox_step(qs, k, v, biases, states, masks):
    live = [c for c, m in enumerate(masks) if not isinstance(m, str)]
    s = {}
    for c in live:
        sc = _dot_nt(qs[c], k) + biases[c]
        s[c] = sc if masks[c] is None else jnp.where(masks[c], sc, -jnp.inf)
    out = list(states)
    for c in live:
        m, l, acc = states[c]
        m_new = jnp.maximum(m, jnp.max(s[c], axis=-1, keepdims=True))
        alpha = jnp.exp2(m - m_new)
        p = jnp.exp2(s[c] - m_new)
        out[c] = (m_new, alpha * l + jnp.sum(p, axis=-1, keepdims=True), alpha * acc + _dot(p.astype(BF16), v))
    return out


def _diag_masks(jj, tk, rc, n_chains, strict):
    masks = []
    for c in range(n_chains):
        lo_row, hi_row = c * rc, (c + 1) * rc - 1
        lo_key, hi_key = jj * tk, (jj + 1) * tk - 1
        if (lo_key >= hi_row) if strict else (lo_key > hi_row):
            masks.append("skip")
        elif (hi_key < lo_row) if strict else (hi_key <= lo_row):
            masks.append(None)
        else:
            rows = lax.broadcasted_iota(jnp.int32, (rc, tk), 0) + lo_row
            cols = lax.broadcasted_iota(jnp.int32, (rc, tk), 1) + lo_key
            masks.append(cols < rows if strict else cols <= rows)
    return masks


def _head_norm(o, g):
    return (o * lax.rsqrt(jnp.mean(o * o, axis=-1, keepdims=True) + EPS) * g).astype(BF16)


def _sb_prompt_kernel(q_ref, k_ref, v_ref, g_ref, o_ref, z_ref, a_ref, acc_ref, *, tq, tk, n_chains):
    qi = pl.program_id(1)
    rc = tq // n_chains
    ratio = tq // tk
    qs = [q_ref[0, 0, c * rc:(c + 1) * rc, :] for c in range(n_chains)]
    suffix = _suffix_ones(tk)

    def kv(j):
        start = pl.multiple_of(j * tk, tk)
        return k_ref[0, 0, pl.ds(start, tk), :], v_ref[0, 0, pl.ds(start, tk), :]

    states = [(jnp.zeros((rc, 1), F32), jnp.zeros((rc, HEAD_DIM), F32)) for _ in range(n_chains)]
    for jj in reversed(range(ratio)):
        k, v = kv(qi * ratio + jj)
        states = _sb_step(qs, k, v, suffix, states, _diag_masks(jj, tk, rc, n_chains, True))

    n_main = qi * ratio
    chains = range(n_chains)

    def put_scores(slot, j):
        k, _ = kv(jnp.maximum(j, 0))
        for c in chains:
            z_ref[slot, c] = _dot_nt(qs[c], k)

    def half_trip(src, dst, j, j_prev, carries):
        put_scores(dst, j - 1)
        _, v_prev = kv(j_prev)
        for c in chains:
            acc_ref[c] += _dot(a_ref[src, c], v_prev)
        z = [z_ref[src, c] for c in chains]
        incl = [_dot(_softplus2(z[c]).astype(BF16), suffix) for c in chains]
        for c in chains:
            a_ref[dst, c] = jnp.exp2(z[c] - incl[c] - carries[c]).astype(BF16)
        return [carries[c] + incl[c][:, :1] for c in chains]

    def body(pair, loop_state):
        j_prev, carries = loop_state
        j = n_main - 1 - 2 * pair
        carries = half_trip(0, 1, j, j_prev, carries)
        carries = half_trip(1, 0, j - 1, j, carries)
        return j - 1, carries

    put_scores(0, n_main - 1)
    for c in chains:
        a_ref[0, c] = jnp.zeros((rc, tk), BF16)
        acc_ref[c] = states[c][1]
    j_last, _ = lax.fori_loop(0, n_main // 2, body, (0, [states[c][0] for c in chains]))
    _, v_last = kv(j_last)
    for c in chains:
        o_ref[c * rc:(c + 1) * rc, :] = _head_norm(acc_ref[c] + _dot(a_ref[0, c], v_last), g_ref[0])


def _fox_prompt_kernel(q_ref, k_ref, v_ref, cq_ref, ck_ref, g_ref, o_ref, z_ref, p_ref, acc_ref, *, tq, tk, n_chains):
    qi = pl.program_id(1)
    rc = tq // n_chains
    ratio = tq // tk
    qs = [q_ref[0, 0, c * rc:(c + 1) * rc, :] for c in range(n_chains)]
    cqs = [cq_ref[0, c * rc:(c + 1) * rc, :] for c in range(n_chains)]

    def kv(j):
        start = pl.multiple_of(j * tk, tk)
        return k_ref[0, 0, pl.ds(start, tk), :], v_ref[0, 0, pl.ds(start, tk), :]

    diag = [kv(qi * ratio + jj) for jj in range(ratio)]
    masks = [_diag_masks(jj, tk, rc, n_chains, False) for jj in range(ratio)]
    states = []
    for c in range(n_chains):
        seen = [jj for jj in range(ratio) if not isinstance(masks[jj][c], str)]
        s = []
        for jj in seen:
            sc = _dot_nt(qs[c], diag[jj][0]) + (cqs[c] - ck_ref[0, qi * ratio + jj])
            s.append(sc if masks[jj][c] is None else jnp.where(masks[jj][c], sc, -jnp.inf))
        m = functools.reduce(jnp.maximum, [jnp.max(x, axis=-1, keepdims=True) for x in s])
        p = [jnp.exp2(x - m) for x in s]
        l = functools.reduce(jnp.add, [jnp.sum(x, axis=-1, keepdims=True) for x in p])
        acc = functools.reduce(jnp.add, [_dot(x.astype(BF16), diag[jj][1]) for x, jj in zip(p, seen)])
        states.append((m, l, acc))

    n_main = qi * ratio
    chains = range(n_chains)

    def put_scores(slot, j):
        k, _ = kv(jnp.maximum(j, 0))
        for c in chains:
            z_ref[slot, c] = _dot_nt(qs[c], k)

    lane = lax.broadcasted_iota(jnp.int32, (1, HEAD_DIM), 1)
    ones_col = jnp.broadcast_to(jnp.where(lane == 0, 1.0, 0.0).astype(BF16), (tk, HEAD_DIM))

    def values(j):
        _, v = kv(j)
        return jnp.concatenate([v, ones_col], axis=1)

    def half_trip(src, dst, j, j_prev, maxes):
        put_scores(dst, j - 1)
        v_prev = values(j_prev)
        ck = ck_ref[0, j]
        out = []
        for c in chains:
            s = z_ref[src, c] + (cqs[c] - ck)
            m_new = jnp.maximum(maxes[c], jnp.max(s, axis=-1, keepdims=True))
            acc_ref[c] = (acc_ref[c] + _dot(p_ref[src, c], v_prev)) * jnp.exp2(maxes[c] - m_new)
            p_ref[dst, c] = jnp.exp2(s - m_new).astype(BF16)
            out.append(m_new)
        return out

    def body(pair, loop_state):
        j_prev, maxes = loop_state
        j = n_main - 1 - 2 * pair
        maxes = half_trip(0, 1, j, j_prev, maxes)
        maxes = half_trip(1, 0, j - 1, j, maxes)
        return j - 1, maxes

    put_scores(0, n_main - 1)
    for c in chains:
        p_ref[0, c] = jnp.zeros((rc, tk), BF16)
        acc_ref[c] = jnp.concatenate([states[c][2], jnp.where(lane == 0, states[c][1], 0.0)], axis=1)
    j_last, _ = lax.fori_loop(0, n_main // 2, body, (0, [states[c][0] for c in chains]))
    v_last = values(j_last)
    for c in chains:
        acc = acc_ref[c] + _dot(p_ref[0, c], v_last)
        o_ref[c * rc:(c + 1) * rc, :] = _head_norm(acc[:, :HEAD_DIM] / acc[:, HEAD_DIM:HEAD_DIM + 1], g_ref[0])


SB_TILES = (2048, 256, 8)
FOX_TILES = (2048, 512, 4)


def _attn_call(kernel_fn, name, tiles, t, acc_lanes, extra_specs):
    tq, tk, n_chains = (min(tiles[0], t), min(tiles[1], t), tiles[2])
    rc = tq // n_chains
    assert (tq // tk) % 2 == 0, "the pipelined sweep walks key blocks in pairs"
    first = 0 if name == "sb_prompt" else 3
    q_spec = pl.BlockSpec((1, 1, tq, HEAD_DIM), lambda h, i: (first, h, i, 0))
    kv_spec = lambda n: pl.BlockSpec((1, 1, t, HEAD_DIM), lambda h, i: (n, h, 0, 0))
    g_spec = pl.BlockSpec((1, 1, HEAD_DIM), lambda h, i: (h, 0, 0))
    return pl.pallas_call(
        functools.partial(kernel_fn, tq=tq, tk=tk, n_chains=n_chains),
        grid=(N_HEADS, t // tq),
        in_specs=[q_spec, kv_spec(first + 1), kv_spec(first + 2)] + extra_specs(tq, tk) + [g_spec],
        out_specs=pl.BlockSpec((tq, HEAD_DIM), lambda h, i: (i, h)),
        out_shape=jax.ShapeDtypeStruct((t, D_GROUP), BF16),
        scratch_shapes=[pltpu.VMEM((2, n_chains, rc, tk), F32), pltpu.VMEM((2, n_chains, rc, tk), BF16),
                        pltpu.VMEM((n_chains, rc, acc_lanes), F32)],
        compiler_params=_params(("parallel", "arbitrary")),
        name=name,
    )


def _prompt_attention(qkv, c_rows, g_sb, g_fx):
    t = qkv.shape[2]
    o_sb = _attn_call(_sb_prompt_kernel, "sb_prompt", SB_TILES, t, HEAD_DIM, lambda tq, tk: [])(qkv, qkv, qkv, g_sb)
    fox_tk = min(FOX_TILES[1], t)
    c_specs = lambda tq, tk: [pl.BlockSpec((1, tq, 1), lambda h, i: (h, i, 0)),
                              pl.BlockSpec((1, t // tk, 1, tk), lambda h, i: (h, 0, 0, 0))]
    o_fx = _attn_call(_fox_prompt_kernel, "fox_prompt", FOX_TILES, t, 2 * HEAD_DIM, c_specs)(
        qkv, qkv, qkv, c_rows.reshape(N_HEADS, t, 1), c_rows.reshape(N_HEADS, t // fox_tk, 1, fox_tk), g_fx)
    return o_sb, o_fx


def _sb_sample_kernel(q_ref, k_ref, v_ref, pk_ref, pv_ref, g_ref, o_ref, *, blk):
    tq = q_ref.shape[3]
    past = pk_ref.shape[3]
    q = q_ref[0, 0, 0]
    states = [(jnp.zeros((tq, 1), F32), jnp.zeros((tq, HEAD_DIM), F32))]
    (carry, acc), = _sb_step([q], k_ref[0, 0, 0], v_ref[0, 0, 0], _suffix_ones(tq), states,
                             _diag_masks(0, tq, tq, 1, True))
    suffix = _suffix_ones(blk)
    blocks = range(past // blk)
    k_blk = lambda j: pk_ref[0, 0, 0, j * blk:(j + 1) * blk, :].astype(BF16)
    v_blk = lambda j: pv_ref[0, 0, 0, j * blk:(j + 1) * blk, :].astype(BF16)
    z = [_dot_nt(q, k_blk(j)) for j in blocks]
    incl = [_dot(_softplus2(z[j]).astype(BF16), suffix) for j in blocks]
    for j in reversed(blocks):
        acc = acc + _dot(jnp.exp2(z[j] - incl[j] - carry).astype(BF16), v_blk(j))
        carry = carry + incl[j][:, :1]
    o_ref[...] = _head_norm(acc, g_ref[0])


def _fox_sample_kernel(q_ref, k_ref, v_ref, pk_ref, pv_ref, cq_ref, cn_ref, cp_ref, g_ref, o_ref, *, blk):
    tq = q_ref.shape[3]
    past = pk_ref.shape[3]
    q = q_ref[0, 0, 0]
    cq = cq_ref[0, 0]
    blocks = range(past // blk)
    k_blk = lambda j: pk_ref[0, 0, 0, j * blk:(j + 1) * blk, :].astype(BF16)
    v_blk = lambda j: pv_ref[0, 0, 0, j * blk:(j + 1) * blk, :].astype(BF16)
    (mask,) = _diag_masks(0, tq, tq, 1, False)
    s_new = jnp.where(mask, _dot_nt(q, k_ref[0, 0, 0]) + (cq - cn_ref[0, 0]), -jnp.inf)
    s = [_dot_nt(q, k_blk(j)) + (cq - cp_ref[0, 0, j]) for j in blocks]
    m = jnp.max(s_new, axis=-1, keepdims=True)
    for j in blocks:
        m = jnp.maximum(m, jnp.max(s[j], axis=-1, keepdims=True))
    p_new = jnp.exp2(s_new - m)
    l = jnp.sum(p_new, axis=-1, keepdims=True)
    acc = _dot(p_new.astype(BF16), v_ref[0, 0, 0])
    for j in blocks:
        p = jnp.exp2(s[j] - m)
        l = l + jnp.sum(p, axis=-1, keepdims=True)
        acc = acc + _dot(p.astype(BF16), v_blk(j))
    o_ref[...] = _head_norm(acc / l, g_ref[0])


def _sample_attention(qkv, past_sb_k, past_sb_v, past_fx_k, past_fx_v, c_all, g_sb, g_fx, blk):
    b, _, past, _ = past_sb_k.shape
    tq = qkv.shape[2] // b
    qkv = qkv.reshape(6, N_HEADS, b, tq, HEAD_DIM)
    new_spec = lambda n: pl.BlockSpec((1, 1, 1, tq, HEAD_DIM), lambda bi, h: (n, h, bi, 0, 0))
    past_spec = pl.BlockSpec((1, 1, 1, past, HEAD_DIM), lambda bi, h: (0, bi, h, 0, 0))
    g_spec = pl.BlockSpec((1, 1, HEAD_DIM), lambda bi, h: (h, 0, 0))
    o_spec = pl.BlockSpec((tq, HEAD_DIM), lambda bi, h: (bi, h))
    o_shape = jax.ShapeDtypeStruct((b * tq, D_GROUP), BF16)
    o_sb = pl.pallas_call(
        functools.partial(_sb_sample_kernel, blk=blk),
        grid=(b, N_HEADS),
        in_specs=[new_spec(0), new_spec(1), new_spec(2), past_spec, past_spec, g_spec],
        out_specs=o_spec, out_shape=o_shape,
        compiler_params=_params(("parallel", "parallel")),
        name="sb_sample",
    )(qkv, qkv, qkv, past_sb_k[None], past_sb_v[None], g_sb)
    n_blk = past // blk
    cq = c_all[:, :, past:].reshape(b, N_HEADS, tq, 1)
    cn = c_all[:, :, past:].reshape(b, N_HEADS, 1, tq)
    cp = c_all[:, :, :past].reshape(b, N_HEADS, n_blk, 1, blk)
    o_fx = pl.pallas_call(
        functools.partial(_fox_sample_kernel, blk=blk),
        grid=(b, N_HEADS),
        in_specs=[new_spec(3), new_spec(4), new_spec(5), past_spec, past_spec,
                  pl.BlockSpec((1, 1, tq, 1), lambda bi, h: (bi, h, 0, 0)),
                  pl.BlockSpec((1, 1, 1, tq), lambda bi, h: (bi, h, 0, 0)),
                  pl.BlockSpec((1, 1, n_blk, 1, blk), lambda bi, h: (bi, h, 0, 0, 0)),
                  g_spec],
        out_specs=o_spec, out_shape=o_shape,
        compiler_params=_params(("parallel", "parallel")),
        name="fox_sample",
    )(qkv, qkv, qkv, past_fx_k[None], past_fx_v[None], cq, cn, cp, g_fx)
    return o_sb, o_fx


def _out_proj_kernel(x_ref, osb_ref, ofx_ref, wsb_ref, wfx_ref, o_ref):
    o_ref[...] = x_ref[...] + _dot(osb_ref[...], wsb_ref[...]) + _dot(ofx_ref[...], wfx_ref[...])


def _out_proj(x, o_sb, o_fx, w_out, tm):
    m, d = x.shape
    row = lambda i: (i, 0)
    return pl.pallas_call(
        _out_proj_kernel,
        grid=(m // tm,),
        in_specs=[pl.BlockSpec((tm, d), row), pl.BlockSpec((tm, D_GROUP), row), pl.BlockSpec((tm, D_GROUP), row),
                  pl.BlockSpec((D_GROUP, d), lambda i: (0, 0)), pl.BlockSpec((D_GROUP, d), lambda i: (1, 0))],
        out_specs=pl.BlockSpec((tm, d), row),
        out_shape=jax.ShapeDtypeStruct((m, d), F32),
        compiler_params=_params(("parallel",)),
        name="out_proj",
    )(x, o_sb, o_fx, w_out, w_out)


def _ple_kernel(x_ref, xn_ref, p_ref, wg_ref, wp_ref, gf_ref, o_ref):
    gate = jax.nn.sigmoid(_dot(xn_ref[...], wg_ref[...]))
    emb = _dot(p_ref[...].astype(BF16), wp_ref[...])
    o_ref[...] = _rms(x_ref[...] + gate * emb, gf_ref[...])


def _ple_final(x, xn, p, w_gate, w_proj, g_final, tm):
    m, d = x.shape
    ple = p.shape[1]
    row = lambda i: (i, 0)
    full = lambda i: (0, 0)
    return pl.pallas_call(
        _ple_kernel,
        grid=(m // tm,),
        in_specs=[pl.BlockSpec((tm, d), row), pl.BlockSpec((tm, d), row), pl.BlockSpec((tm, ple), row),
                  pl.BlockSpec((d, d), full), pl.BlockSpec((ple, d), full), pl.BlockSpec((1, d), full)],
        out_specs=pl.BlockSpec((tm, d), row),
        out_shape=jax.ShapeDtypeStruct((m, d), F32),
        compiler_params=_params(("parallel",)),
        name="ple_final",
    )(x, xn, p, w_gate, w_proj, g_final)


def _pad_lanes(x, n):
    return jnp.pad(x, ((0, 0), (0, n - x.shape[1])))


def _run_group(x, p, past, w, attn_blk):
    b, t, d = x.shape
    m = b * t
    tm = min(512, m)
    row = lambda g: g.reshape(1, -1)
    x0 = x.reshape(m, d)
    x1, xn = _ffn(x0, row(w["g_ffn1"]), w["ffn1_gate"], w["ffn1_up"], w["ffn1_down"], row(w["g_mix"]), tm, 512)
    qkv, k_sb, v_sb, k_fx, v_fx, lf_cols = _proj(xn, w["in_qkv"], w["in_f"], w["b_f"], tm)
    lf = lf_cols[:, :N_HEADS].reshape(b, t, N_HEADS).transpose(0, 2, 1)
    g_sb = w["g_out_sb"].reshape(N_HEADS, 1, HEAD_DIM)
    g_fx = w["g_out_fox"].reshape(N_HEADS, 1, HEAD_DIM)
    if past is None:
        c_rows = _cumsum_rows(lf.reshape(N_HEADS, t))
        o_sb, o_fx = _prompt_attention(qkv, c_rows, g_sb, g_fx)
    else:
        lf_all = jnp.concatenate([past[4], lf], axis=-1).reshape(b * N_HEADS, -1)
        total = lf_all.shape[1]
        padded = -(-total // LANES) * LANES
        c_all = _cumsum_rows(_pad_lanes(lf_all, padded))[:, :total].reshape(b, N_HEADS, total)
        o_sb, o_fx = _sample_attention(qkv, past[0], past[1], past[2], past[3], c_all, g_sb, g_fx, attn_blk)
    x2 = _out_proj(x1, o_sb, o_fx, w["out"], tm)
    x3, xn3 = _ffn(x2, row(w["g_ffn2"]), w["ffn2_gate"], w["ffn2_up"], w["ffn2_down"], row(w["g_ple"]), tm, 512)
    y = _ple_final(x3, xn3, p.reshape(m, -1), w["ple_gate"], w["ple_proj"], row(w["g_final"]), min(256, m))
    heads = lambda a: a.reshape(N_HEADS, b, t, HEAD_DIM).transpose(1, 0, 2, 3)[None]
    return y.reshape(b, t, d), heads(k_sb), heads(v_sb), heads(k_fx), heads(v_fx), lf[None]


def kernel(x_prompt, x_sample, cache_sb_k, cache_sb_v, cache_fox_k, cache_fox_v, cache_fox_logf, p_prompt, p_sample, g_ffn1, w_ffn1_gate, w_ffn1_up, w_ffn1_down, g_mix, w_in, b_forget, g_out_sb, g_out_fox, w_out, g_ffn2, w_ffn2_gate, w_ffn2_up, w_ffn2_down, g_ple, w_ple_gate, w_ple_proj, g_final):
    assert g_ffn1.shape[0] == 1, "single-layer trunk"
    bf = lambda a: a.astype(BF16)
    n_qkv = 6 * D_GROUP
    w = {
        "g_ffn1": g_ffn1[0], "ffn1_gate": bf(w_ffn1_gate[0]), "ffn1_up": bf(w_ffn1_up[0]), "ffn1_down": bf(w_ffn1_down[0]),
        "g_mix": g_mix[0], "in_qkv": bf(w_in[0]),
        "in_f": _pad_lanes(bf(w_in[0][:, n_qkv:]), LANES), "b_f": _pad_lanes(b_forget[0][None, :], LANES),
        "g_out_sb": g_out_sb[0], "g_out_fox": g_out_fox[0],
        "out": bf(w_out[0]),
        "g_ffn2": g_ffn2[0], "ffn2_gate": bf(w_ffn2_gate[0]), "ffn2_up": bf(w_ffn2_up[0]), "ffn2_down": bf(w_ffn2_down[0]),
        "g_ple": g_ple[0], "ple_gate": bf(w_ple_gate[0]), "ple_proj": bf(w_ple_proj[0]), "g_final": g_final,
    }
    past = (cache_sb_k[0], cache_sb_v[0], cache_fox_k[0], cache_fox_v[0], cache_fox_logf[0])
    y_p, sb_k_p, sb_v_p, fx_k_p, fx_v_p, lf_p = _run_group(x_prompt, p_prompt[0], None, w, 256)
    y_s, sb_k_s, sb_v_s, fx_k_s, fx_v_s, lf_s = _run_group(x_sample, p_sample[0], past, w, 256)
    return (y_p, y_s, sb_k_p, sb_v_p, fx_k_p, fx_v_p, lf_p, sb_k_s, sb_v_s, fx_k_s, fx_v_s, lf_s)
```

```python
import functools
import math

import jax
import jax.numpy as jnp
from jax import lax
from jax.experimental import pallas as pl
from jax.experimental.pallas import tpu as pltpu

HEAD_DIM = 128
N_HEADS = 8
D_GROUP = N_HEADS * HEAD_DIM
MACARON = 0.5
EPS = 1e-6
SCALE = 1.0 / math.sqrt(HEAD_DIM)
LOG2E = math.log2(math.e)
LANES = 128
V7X_VMEM_LIMIT = 56 * 1024 * 1024

F32 = jnp.float32
BF16 = jnp.bfloat16


def _params(semantics):
    return pltpu.CompilerParams(dimension_semantics=semantics, vmem_limit_bytes=V7X_VMEM_LIMIT)


def _rms(x, g):
    return x * lax.rsqrt(jnp.mean(x * x, axis=-1, keepdims=True) + EPS) * g


def _dot(a, b):
    return jnp.dot(a, b, preferred_element_type=F32)


def _dot_nt(a, b):
    return lax.dot_general(a, b, (((1,), (1,)), ((), ())), preferred_element_type=F32)


def _split3(x):
    hi = x.astype(BF16)
    r = x - hi.astype(F32)
    mid = r.astype(BF16)
    lo = (r - mid.astype(F32)).astype(BF16)
    return hi, mid, lo


def _softplus2(z2):
    return jnp.maximum(z2, 0.0) + jnp.log(1.0 + jnp.exp2(-jnp.abs(z2))) * LOG2E


def _ffn_kernel(x_ref, g_ref, wg_ref, wu_ref, wd_ref, gn_ref, o_ref, on_ref, xn_ref):
    f = pl.program_id(1)

    @pl.when(f == 0)
    def _():
        xn_ref[...] = _rms(x_ref[...], g_ref[...]).astype(BF16)
        o_ref[...] = jnp.zeros_like(o_ref)

    xn = xn_ref[...]
    gate = _dot(xn, wg_ref[...])
    up = _dot(xn, wu_ref[...])
    h = (gate * jax.nn.sigmoid(gate) * up).astype(BF16)
    o_ref[...] += _dot(h, wd_ref[...])

    @pl.when(f == pl.num_programs(1) - 1)
    def _():
        y = x_ref[...] + MACARON * o_ref[...]
        o_ref[...] = y
        on_ref[...] = _rms(y, gn_ref[...]).astype(BF16)


def _ffn(x, g, wg, wu, wd, g_next, tm, tf):
    m, d = x.shape
    d_ff = wg.shape[1]
    row = lambda i, f: (i, 0)
    return pl.pallas_call(
        _ffn_kernel,
        grid=(m // tm, d_ff // tf),
        in_specs=[
            pl.BlockSpec((tm, d), row),
            pl.BlockSpec((1, d), lambda i, f: (0, 0)),
            pl.BlockSpec((d, tf), lambda i, f: (0, f)),
            pl.BlockSpec((d, tf), lambda i, f: (0, f)),
            pl.BlockSpec((tf, d), lambda i, f: (f, 0)),
            pl.BlockSpec((1, d), lambda i, f: (0, 0)),
        ],
        out_specs=[pl.BlockSpec((tm, d), row), pl.BlockSpec((tm, d), row)],
        out_shape=[jax.ShapeDtypeStruct((m, d), F32), jax.ShapeDtypeStruct((m, d), BF16)],
        scratch_shapes=[pltpu.VMEM((tm, d), BF16)],
        compiler_params=_params(("parallel", "arbitrary")),
        name="ffn",
    )(x, g, wg, wu, wd, g_next)


def _proj_kernel(xn_ref, w_ref, wf_ref, bf_ref, qkv_ref, ksb_ref, vsb_ref, kfx_ref, vfx_ref, lf_ref):
    n = pl.program_id(1)
    xn = xn_ref[...]
    res = _dot(xn, w_ref[...])
    is_q = jnp.logical_or(n == 0, n == 3)
    res_mx = res * jnp.where(is_q, SCALE * LOG2E, 1.0)
    for h in range(N_HEADS):
        qkv_ref[0, h] = res_mx[:, h * HEAD_DIM:(h + 1) * HEAD_DIM].astype(BF16)

    def store_heads(ref):
        for h in range(N_HEADS):
            ref[h] = res[:, h * HEAD_DIM:(h + 1) * HEAD_DIM]

    for idx, ref in ((1, ksb_ref), (2, vsb_ref), (4, kfx_ref), (5, vfx_ref)):
        pl.when(n == idx)(functools.partial(store_heads, ref))

    @pl.when(n == 0)
    def _():
        fz = _dot(xn, wf_ref[...]) + bf_ref[...]
        lf_ref[...] = jnp.minimum(fz, 0.0) - jnp.log1p(jnp.exp(-jnp.abs(fz)))


def _proj(xn, w_qkv, w_f, b_f, tm):
    m, d = xn.shape
    head_f32 = jax.ShapeDtypeStruct((N_HEADS, m, HEAD_DIM), F32)
    head_spec = pl.BlockSpec((N_HEADS, tm, HEAD_DIM), lambda i, n: (0, i, 0))
    return pl.pallas_call(
        _proj_kernel,
        grid=(m // tm, 6),
        in_specs=[
            pl.BlockSpec((tm, d), lambda i, n: (i, 0)),
            pl.BlockSpec((d, D_GROUP), lambda i, n: (0, n)),
            pl.BlockSpec((d, LANES), lambda i, n: (0, 0)),
            pl.BlockSpec((1, LANES), lambda i, n: (0, 0)),
        ],
        out_specs=[
            pl.BlockSpec((1, N_HEADS, tm, HEAD_DIM), lambda i, n: (n, 0, i, 0)),
            head_spec, head_spec, head_spec, head_spec,
            pl.BlockSpec((tm, LANES), lambda i, n: (i, 0)),
        ],
        out_shape=[
            jax.ShapeDtypeStruct((6, N_HEADS, m, HEAD_DIM), BF16),
            head_f32, head_f32, head_f32, head_f32,
            jax.ShapeDtypeStruct((m, LANES), F32),
        ],
        compiler_params=_params(("parallel", "arbitrary")),
        name="in_proj",
    )(xn, w_qkv, w_f, b_f)


def _cumsum_kernel(x_ref, o_ref):
    n_chunks, rows, _ = x_ref.shape
    upper = (lax.broadcasted_iota(jnp.int32, (LANES, LANES), 0)
             <= lax.broadcasted_iota(jnp.int32, (LANES, LANES), 1)).astype(BF16)

    def body(c, carry):
        hi, mid, lo = _split3(x_ref[c])
        cs = _dot(hi, upper) + _dot(mid, upper) + _dot(lo, upper) + carry
        o_ref[c] = cs * LOG2E
        return cs[:, LANES - 1:LANES]

    lax.fori_loop(0, n_chunks, body, jnp.zeros((rows, 1), F32))


def _cumsum_rows(x):
    rows, length = x.shape
    xc = x.reshape(rows, length // LANES, LANES).transpose(1, 0, 2)
    out = pl.pallas_call(
        _cumsum_kernel,
        out_shape=jax.ShapeDtypeStruct(xc.shape, F32),
        compiler_params=pltpu.CompilerParams(vmem_limit_bytes=V7X_VMEM_LIMIT),
        name="cumsum",
    )(xc)
    return out.transpose(1, 0, 2).reshape(rows, length)


def _suffix_ones(n):
    return (lax.broadcasted_iota(jnp.int32, (n, n), 0)
            >= lax.broadcasted_iota(jnp.int32, (n, n), 1)).astype(BF16)


def _sb_step(qs, k, v, suffix, states, masks):
    live = [c for c, m in enumerate(masks) if not isinstance(m, str)]
    z = {c: _dot_nt(qs[c], k) for c in live}
    sp = {}
    for c in live:
        s = _softplus2(z[c])
        sp[c] = s if masks[c] is None else jnp.where(masks[c], s, 0.0)
    incl = {c: _dot(sp[c].astype(BF16), suffix) for c in live}
    out = list(states)
    for c in live:
        carry, acc = states[c]
        a = jnp.exp2(z[c] - incl[c] - carry)
        if masks[c] is not None:
            a = jnp.where(masks[c], a, 0.0)
        out[c] = (carry + incl[c][:, :1], acc + _dot(a.astype(BF16), v))
    return out


def _fox_step(qs, k, v, biases, states, masks):
    live = [c for c, m in enumerate(masks) if not isinstance(m, str)]
    s = {}
    for c in live:
        sc = _dot_nt(qs[c], k) + biases[c]
        s[c] = sc if masks[c] is None else jnp.where(masks[c], sc, -jnp.inf)
    out = list(states)
    for c in live:
        m, l, acc = states[c]
        m_new = jnp.maximum(m, jnp.max(s[c], axis=-1, keepdims=True))
        alpha = jnp.exp2(m - m_new)
        p = jnp.exp2(s[c] - m_new)
        out[c] = (m_new, alpha * l + jnp.sum(p, axis=-1, keepdims=True), alpha * acc + _dot(p.astype(BF16), v))
    return out


def _diag_masks(jj, tk, rc, n_chains, strict):
    masks = []
    for c in range(n_chains):
        lo_row, hi_row = c * rc, (c + 1) * rc - 1
        lo_key, hi_key = jj * tk, (jj + 1) * tk - 1
        if (lo_key >= hi_row) if strict else (lo_key > hi_row):
            masks.append("skip")
        elif (hi_key < lo_row) if strict else (hi_key <= lo_row):
            masks.append(None)
        else:
            rows = lax.broadcasted_iota(jnp.int32, (rc, tk), 0) + lo_row
            cols = lax.broadcasted_iota(jnp.int32, (rc, tk), 1) + lo_key
            masks.append(cols < rows if strict else cols <= rows)
    return masks


def _head_norm(o, g):
    return (o * lax.rsqrt(jnp.mean(o * o, axis=-1, keepdims=True) + EPS) * g).astype(BF16)


def _sb_prompt_kernel(q_ref, k_ref, v_ref, g_ref, o_ref, z_ref, a_ref, acc_ref, *, tq, tk, n_chains):
    qi = pl.program_id(1)
    rc = tq // n_chains
    ratio = tq // tk
    qs = [q_ref[0, 0, c * rc:(c + 1) * rc, :] for c in range(n_chains)]
    suffix = _suffix_ones(tk)

    def kv(j):
        start = pl.multiple_of(j * tk, tk)
        return k_ref[0, 0, pl.ds(start, tk), :], v_ref[0, 0, pl.ds(start, tk), :]

    states = [(jnp.zeros((rc, 1), F32), jnp.zeros((rc, HEAD_DIM), F32)) for _ in range(n_chains)]
    for jj in reversed(range(ratio)):
        k, v = kv(qi * ratio + jj)
        states = _sb_step(qs, k, v, suffix, states, _diag_masks(jj, tk, rc, n_chains, True))

    n_main = qi * ratio
    chains = range(n_chains)

    def put_scores(slot, j):
        k, _ = kv(jnp.maximum(j, 0))
        for c in chains:
            z_ref[slot, c] = _dot_nt(qs[c], k)

    def half_trip(src, dst, j, j_prev, carries):
        put_scores(dst, j - 1)
        _, v_prev = kv(j_prev)
        for c in chains:
            acc_ref[c] += _dot(a_ref[src, c], v_prev)
        z = [z_ref[src, c] for c in chains]
        incl = [_dot(_softplus2(z[c]).astype(BF16), suffix) for c in chains]
        for c in chains:
            a_ref[dst, c] = jnp.exp2(z[c] - incl[c] - carries[c]).astype(BF16)
        return [carries[c] + incl[c][:, :1] for c in chains]

    def body(pair, loop_state):
        j_prev, carries = loop_state
        j = n_main - 1 - 2 * pair
        carries = half_trip(0, 1, j, j_prev, carries)
        carries = half_trip(1, 0, j - 1, j, carries)
        return j - 1, carries

    put_scores(0, n_main - 1)
    for c in chains:
        a_ref[0, c] = jnp.zeros((rc, tk), BF16)
        acc_ref[c] = states[c][1]
    j_last, _ = lax.fori_loop(0, n_main // 2, body, (0, [states[c][0] for c in chains]))
    _, v_last = kv(j_last)
    for c in chains:
        o_ref[c * rc:(c + 1) * rc, :] = _head_norm(acc_ref[c] + _dot(a_ref[0, c], v_last), g_ref[0])


def _fox_prompt_kernel(q_ref, k_ref, v_ref, cq_ref, ck_ref, g_ref, o_ref, z_ref, p_ref, acc_ref, *, tq, tk, n_chains):
    qi = pl.program_id(1)
    rc = tq // n_chains
    ratio = tq // tk
    qs = [q_ref[0, 0, c * rc:(c + 1) * rc, :] for c in range(n_chains)]
    cqs = [cq_ref[0, c * rc:(c + 1) * rc, :] for c in range(n_chains)]

    def kv(j):
        start = pl.multiple_of(j * tk, tk)
        return k_ref[0, 0, pl.ds(start, tk), :], v_ref[0, 0, pl.ds(start, tk), :]

    lane = lax.broadcasted_iota(jnp.int32, (1, HEAD_DIM), 1)
    ones_col = jnp.broadcast_to(jnp.where(lane == 0, 1.0, 0.0).astype(BF16), (tk, HEAD_DIM))

    def values(j):
        _, v = kv(j)
        return jnp.concatenate([v, ones_col], axis=1)

    masks = [_diag_masks(jj, tk, rc, n_chains, False) for jj in range(ratio)]
    states = []
    for c in range(n_chains):
        seen = [jj for jj in range(ratio) if not isinstance(masks[jj][c], str)]
        s = []
        for jj in seen:
            sc = _dot_nt(qs[c], kv(qi * ratio + jj)[0]) + (cqs[c] - ck_ref[0, qi * ratio + jj])
            s.append(sc if masks[jj][c] is None else jnp.where(masks[jj][c], sc, -jnp.inf))
        m = functools.reduce(jnp.maximum, [jnp.max(x, axis=-1, keepdims=True) for x in s])
        acc = functools.reduce(jnp.add, [_dot(jnp.exp2(x - m).astype(BF16), values(qi * ratio + jj))
                                         for x, jj in zip(s, seen)])
        states.append((m, acc))

    n_main = qi * ratio
    chains = range(n_chains)

    def put_scores(slot, j):
        k, _ = kv(jnp.maximum(j, 0))
        for c in chains:
            z_ref[slot, c] = _dot_nt(qs[c], k)

    def half_trip(src, dst, j, j_prev, maxes):
        put_scores(dst, j - 1)
        v_prev = values(j_prev)
        ck = ck_ref[0, j]
        out = []
        for c in chains:
            s = z_ref[src, c] + (cqs[c] - ck)
            m_new = jnp.maximum(maxes[c], jnp.max(s, axis=-1, keepdims=True))
            acc_ref[c] = (acc_ref[c] + _dot(p_ref[src, c], v_prev)) * jnp.exp2(maxes[c] - m_new)
            p_ref[dst, c] = jnp.exp2(s - m_new).astype(BF16)
            out.append(m_new)
        return out

    def body(pair, loop_state):
        j_prev, maxes = loop_state
        j = n_main - 1 - 2 * pair
        maxes = half_trip(0, 1, j, j_prev, maxes)
        maxes = half_trip(1, 0, j - 1, j, maxes)
        return j - 1, maxes

    put_scores(0, n_main - 1)
    for c in chains:
        p_ref[0, c] = jnp.zeros((rc, tk), BF16)
        acc_ref[c] = states[c][1]
    j_last, _ = lax.fori_loop(0, n_main // 2, body, (0, [states[c][0] for c in chains]))
    v_last = values(j_last)
    for c in chains:
        acc = acc_ref[c] + _dot(p_ref[0, c], v_last)
        o_ref[c * rc:(c + 1) * rc, :] = _head_norm(acc[:, :HEAD_DIM] / acc[:, HEAD_DIM:HEAD_DIM + 1], g_ref[0])


SB_TILES = (2048, 256, 8)
FOX_TILES = (2048, 512, 4)


def _attn_call(kernel_fn, name, tiles, t, acc_lanes, extra_specs):
    tq, tk, n_chains = (min(tiles[0], t), min(tiles[1], t), tiles[2])
    rc = tq // n_chains
    assert (tq // tk) % 2 == 0, "the pipelined sweep walks key blocks in pairs"
    first = 0 if name == "sb_prompt" else 3
    q_spec = pl.BlockSpec((1, 1, tq, HEAD_DIM), lambda h, i: (first, h, i, 0))
    kv_spec = lambda n: pl.BlockSpec((1, 1, t, HEAD_DIM), lambda h, i: (n, h, 0, 0))
    g_spec = pl.BlockSpec((1, 1, HEAD_DIM), lambda h, i: (h, 0, 0))
    return pl.pallas_call(
        functools.partial(kernel_fn, tq=tq, tk=tk, n_chains=n_chains),
        grid=(N_HEADS, t // tq),
        in_specs=[q_spec, kv_spec(first + 1), kv_spec(first + 2)] + extra_specs(tq, tk) + [g_spec],
        out_specs=pl.BlockSpec((tq, HEAD_DIM), lambda h, i: (i, h)),
        out_shape=jax.ShapeDtypeStruct((t, D_GROUP), BF16),
        scratch_shapes=[pltpu.VMEM((2, n_chains, rc, tk), F32), pltpu.VMEM((2, n_chains, rc, tk), BF16),
                        pltpu.VMEM((n_chains, rc, acc_lanes), F32)],
        compiler_params=_params(("parallel", "arbitrary")),
        name=name,
    )


def _prompt_attention(qkv, c_rows, g_sb, g_fx):
    t = qkv.shape[2]
    o_sb = _attn_call(_sb_prompt_kernel, "sb_prompt", SB_TILES, t, HEAD_DIM, lambda tq, tk: [])(qkv, qkv, qkv, g_sb)
    fox_tk = min(FOX_TILES[1], t)
    c_specs = lambda tq, tk: [pl.BlockSpec((1, tq, 1), lambda h, i: (h, i, 0)),
                              pl.BlockSpec((1, t // tk, 1, tk), lambda h, i: (h, 0, 0, 0))]
    o_fx = _attn_call(_fox_prompt_kernel, "fox_prompt", FOX_TILES, t, 2 * HEAD_DIM, c_specs)(
        qkv, qkv, qkv, c_rows.reshape(N_HEADS, t, 1), c_rows.reshape(N_HEADS, t // fox_tk, 1, fox_tk), g_fx)
    return o_sb, o_fx


def _sb_sample_kernel(q_ref, k_ref, v_ref, pk_ref, pv_ref, g_ref, o_ref, *, blk):
    tq = q_ref.shape[3]
    past = pk_ref.shape[3]
    suffix_new, suffix = _suffix_ones(tq), _suffix_ones(blk)
    blocks = range(past // blk)
    for hh in range(q_ref.shape[1]):
        q = q_ref[0, hh, 0]
        states = [(jnp.zeros((tq, 1), F32), jnp.zeros((tq, HEAD_DIM), F32))]
        (carry, acc), = _sb_step([q], k_ref[0, hh, 0], v_ref[0, hh, 0], suffix_new, states,
                                 _diag_masks(0, tq, tq, 1, True))
        k_blk = lambda j: pk_ref[0, 0, hh, j * blk:(j + 1) * blk, :].astype(BF16)
        v_blk = lambda j: pv_ref[0, 0, hh, j * blk:(j + 1) * blk, :].astype(BF16)
        z = [_dot_nt(q, k_blk(j)) for j in blocks]
        incl = [_dot(_softplus2(z[j]).astype(BF16), suffix) for j in blocks]
        for j in reversed(blocks):
            acc = acc + _dot(jnp.exp2(z[j] - incl[j] - carry).astype(BF16), v_blk(j))
            carry = carry + incl[j][:, :1]
        o_ref[:, hh * HEAD_DIM:(hh + 1) * HEAD_DIM] = _head_norm(acc, g_ref[hh])


def _fox_sample_kernel(q_ref, k_ref, v_ref, pk_ref, pv_ref, cq_ref, cn_ref, cp_ref, g_ref, o_ref, *, blk):
    tq = q_ref.shape[3]
    past = pk_ref.shape[3]
    blocks = range(past // blk)
    (mask,) = _diag_masks(0, tq, tq, 1, False)
    for hh in range(q_ref.shape[1]):
        q = q_ref[0, hh, 0]
        cq = cq_ref[0, hh]
        k_blk = lambda j: pk_ref[0, 0, hh, j * blk:(j + 1) * blk, :].astype(BF16)
        v_blk = lambda j: pv_ref[0, 0, hh, j * blk:(j + 1) * blk, :].astype(BF16)
        s_new = jnp.where(mask, _dot_nt(q, k_ref[0, hh, 0]) + (cq - cn_ref[0, hh]), -jnp.inf)
        s = [_dot_nt(q, k_blk(j)) + (cq - cp_ref[0, hh, j]) for j in blocks]
        m = jnp.max(s_new, axis=-1, keepdims=True)
        for j in blocks:
            m = jnp.maximum(m, jnp.max(s[j], axis=-1, keepdims=True))
        p_new = jnp.exp2(s_new - m)
        l = jnp.sum(p_new, axis=-1, keepdims=True)
        acc = _dot(p_new.astype(BF16), v_ref[0, hh, 0])
        for j in blocks:
            p = jnp.exp2(s[j] - m)
            l = l + jnp.sum(p, axis=-1, keepdims=True)
            acc = acc + _dot(p.astype(BF16), v_blk(j))
        o_ref[:, hh * HEAD_DIM:(hh + 1) * HEAD_DIM] = _head_norm(acc / l, g_ref[hh])


SAMPLE_HEADS_PER_STEP = 2


def _sample_attention(qkv, past_sb_k, past_sb_v, past_fx_k, past_fx_v, c_all, g_sb, g_fx, blk):
    b, _, past, _ = past_sb_k.shape
    tq = qkv.shape[2] // b
    hp = SAMPLE_HEADS_PER_STEP
    grid = (b, N_HEADS // hp)
    qkv = qkv.reshape(6, N_HEADS, b, tq, HEAD_DIM)
    new_spec = lambda n: pl.BlockSpec((1, hp, 1, tq, HEAD_DIM), lambda bi, h: (n, h, bi, 0, 0))
    past_spec = pl.BlockSpec((1, 1, hp, past, HEAD_DIM), lambda bi, h: (0, bi, h, 0, 0))
    g_spec = pl.BlockSpec((hp, 1, HEAD_DIM), lambda bi, h: (h, 0, 0))
    o_spec = pl.BlockSpec((tq, hp * HEAD_DIM), lambda bi, h: (bi, h))
    o_shape = jax.ShapeDtypeStruct((b * tq, D_GROUP), BF16)
    o_sb = pl.pallas_call(
        functools.partial(_sb_sample_kernel, blk=blk),
        grid=grid,
        in_specs=[new_spec(0), new_spec(1), new_spec(2), past_spec, past_spec, g_spec],
        out_specs=o_spec, out_shape=o_shape,
        compiler_params=_params(("parallel", "parallel")),
        name="sb_sample",
    )(qkv, qkv, qkv, past_sb_k[None], past_sb_v[None], g_sb)
    n_blk = past // blk
    cq = c_all[:, :, past:].reshape(b, N_HEADS, tq, 1)
    cn = c_all[:, :, past:].reshape(b, N_HEADS, 1, tq)
    cp = c_all[:, :, :past].reshape(b, N_HEADS, n_blk, 1, blk)
    o_fx = pl.pallas_call(
        functools.partial(_fox_sample_kernel, blk=blk),
        grid=grid,
        in_specs=[new_spec(3), new_spec(4), new_spec(5), past_spec, past_spec,
                  pl.BlockSpec((1, hp, tq, 1), lambda bi, h: (bi, h, 0, 0)),
                  pl.BlockSpec((1, hp, 1, tq), lambda bi, h: (bi, h, 0, 0)),
                  pl.BlockSpec((1, hp, n_blk, 1, blk), lambda bi, h: (bi, h, 0, 0, 0)),
                  g_spec],
        out_specs=o_spec, out_shape=o_shape,
        compiler_params=_params(("parallel", "parallel")),
        name="fox_sample",
    )(qkv, qkv, qkv, past_fx_k[None], past_fx_v[None], cq, cn, cp, g_fx)
    return o_sb, o_fx


def _out_proj_kernel(x_ref, osb_ref, ofx_ref, wsb_ref, wfx_ref, o_ref):
    o_ref[...] = x_ref[...] + _dot(osb_ref[...], wsb_ref[...]) + _dot(ofx_ref[...], wfx_ref[...])


def _out_proj(x, o_sb, o_fx, w_out, tm):
    m, d = x.shape
    row = lambda i: (i, 0)
    return pl.pallas_call(
        _out_proj_kernel,
        grid=(m // tm,),
        in_specs=[pl.BlockSpec((tm, d), row), pl.BlockSpec((tm, D_GROUP), row), pl.BlockSpec((tm, D_GROUP), row),
                  pl.BlockSpec((D_GROUP, d), lambda i: (0, 0)), pl.BlockSpec((D_GROUP, d), lambda i: (1, 0))],
        out_specs=pl.BlockSpec((tm, d), row),
        out_shape=jax.ShapeDtypeStruct((m, d), F32),
        compiler_params=_params(("parallel",)),
        name="out_proj",
    )(x, o_sb, o_fx, w_out, w_out)


def _ple_kernel(x_ref, xn_ref, p_ref, wg_ref, wp_ref, gf_ref, o_ref):
    gate = jax.nn.sigmoid(_dot(xn_ref[...], wg_ref[...]))
    emb = _dot(p_ref[...].astype(BF16), wp_ref[...])
    o_ref[...] = _rms(x_ref[...] + gate * emb, gf_ref[...])


def _ple_final(x, xn, p, w_gate, w_proj, g_final, tm):
    m, d = x.shape
    ple = p.shape[1]
    row = lambda i: (i, 0)
    full = lambda i: (0, 0)
    return pl.pallas_call(
        _ple_kernel,
        grid=(m // tm,),
        in_specs=[pl.BlockSpec((tm, d), row), pl.BlockSpec((tm, d), row), pl.BlockSpec((tm, ple), row),
                  pl.BlockSpec((d, d), full), pl.BlockSpec((ple, d), full), pl.BlockSpec((1, d), full)],
        out_specs=pl.BlockSpec((tm, d), row),
        out_shape=jax.ShapeDtypeStruct((m, d), F32),
        compiler_params=_params(("parallel",)),
        name="ple_final",
    )(x, xn, p, w_gate, w_proj, g_final)


def _pad_lanes(x, n):
    return jnp.pad(x, ((0, 0), (0, n - x.shape[1])))


def _run_group(x, p, past, w, attn_blk):
    b, t, d = x.shape
    m = b * t
    tm = min(512, m)
    row = lambda g: g.reshape(1, -1)
    x0 = x.reshape(m, d)
    x1, xn = _ffn(x0, row(w["g_ffn1"]), w["ffn1_gate"], w["ffn1_up"], w["ffn1_down"], row(w["g_mix"]), tm, 512)
    qkv, k_sb, v_sb, k_fx, v_fx, lf_cols = _proj(xn, w["in_qkv"], w["in_f"], w["b_f"], tm)
    lf = lf_cols[:, :N_HEADS].reshape(b, t, N_HEADS).transpose(0, 2, 1)
    g_sb = w["g_out_sb"].reshape(N_HEADS, 1, HEAD_DIM)
    g_fx = w["g_out_fox"].reshape(N_HEADS, 1, HEAD_DIM)
    if past is None:
        c_rows = _cumsum_rows(lf.reshape(N_HEADS, t))
        o_sb, o_fx = _prompt_attention(qkv, c_rows, g_sb, g_fx)
    else:
        lf_all = jnp.concatenate([past[4], lf], axis=-1).reshape(b * N_HEADS, -1)
        total = lf_all.shape[1]
        padded = -(-total // LANES) * LANES
        c_all = _cumsum_rows(_pad_lanes(lf_all, padded))[:, :total].reshape(b, N_HEADS, total)
        o_sb, o_fx = _sample_attention(qkv, past[0], past[1], past[2], past[3], c_all, g_sb, g_fx, attn_blk)
    x2 = _out_proj(x1, o_sb, o_fx, w["out"], tm)
    x3, xn3 = _ffn(x2, row(w["g_ffn2"]), w["ffn2_gate"], w["ffn2_up"], w["ffn2_down"], row(w["g_ple"]), tm, 512)
    y = _ple_final(x3, xn3, p.reshape(m, -1), w["ple_gate"], w["ple_proj"], row(w["g_final"]), min(256, m))
    heads = lambda a: a.reshape(N_HEADS, b, t, HEAD_DIM).transpose(1, 0, 2, 3)[None]
    return y.reshape(b, t, d), heads(k_sb), heads(v_sb), heads(k_fx), heads(v_fx), lf[None]


def kernel(x_prompt, x_sample, cache_sb_k, cache_sb_v, cache_fox_k, cache_fox_v, cache_fox_logf, p_prompt, p_sample, g_ffn1, w_ffn1_gate, w_ffn1_up, w_ffn1_down, g_mix, w_in, b_forget, g_out_sb, g_out_fox, w_out, g_ffn2, w_ffn2_gate, w_ffn2_up, w_ffn2_down, g_ple, w_ple_gate, w_ple_proj, g_final):
    assert g_ffn1.shape[0] == 1, "single-layer trunk"
    bf = lambda a: a.astype(BF16)
    n_qkv = 6 * D_GROUP
    w = {
        "g_ffn1": g_ffn1[0], "ffn1_gate": bf(w_ffn1_gate[0]), "ffn1_up": bf(w_ffn1_up[0]), "ffn1_down": bf(w_ffn1_down[0]),
        "g_mix": g_mix[0], "in_qkv": bf(w_in[0]),
        "in_f": _pad_lanes(bf(w_in[0][:, n_qkv:]), LANES), "b_f": _pad_lanes(b_forget[0][None, :], LANES),
        "g_out_sb": g_out_sb[0], "g_out_fox": g_out_fox[0],
        "out": bf(w_out[0]),
        "g_ffn2": g_ffn2[0], "ffn2_gate": bf(w_ffn2_gate[0]), "ffn2_up": bf(w_ffn2_up[0]), "ffn2_down": bf(w_ffn2_down[0]),
        "g_ple": g_ple[0], "ple_gate": bf(w_ple_gate[0]), "ple_proj": bf(w_ple_proj[0]), "g_final": g_final,
    }
    past = (cache_sb_k[0], cache_sb_v[0], cache_fox_k[0], cache_fox_v[0], cache_fox_logf[0])
    y_p, sb_k_p, sb_v_p, fx_k_p, fx_v_p, lf_p = _run_group(x_prompt, p_prompt[0], None, w, 256)
    y_s, sb_k_s, sb_v_s, fx_k_s, fx_v_s, lf_s = _run_group(x_sample, p_sample[0], past, w, 256)
    return (y_p, y_s, sb_k_p, sb_v_p, fx_k_p, fx_v_p, lf_p, sb_k_s, sb_v_s, fx_k_s, fx_v_s, lf_s)
```

```python
import functools
import math

import jax
import jax.numpy as jnp
from jax import lax
from jax.experimental import pallas as pl
from jax.experimental.pallas import tpu as pltpu

HEAD_DIM = 128
N_HEADS = 8
D_GROUP = N_HEADS * HEAD_DIM
MACARON = 0.5
EPS = 1e-6
SCALE = 1.0 / math.sqrt(HEAD_DIM)
LOG2E = math.log2(math.e)
LANES = 128
V7X_VMEM_LIMIT = 56 * 1024 * 1024

F32 = jnp.float32
BF16 = jnp.bfloat16


def _params(semantics):
    return pltpu.CompilerParams(dimension_semantics=semantics, vmem_limit_bytes=V7X_VMEM_LIMIT)


def _rms(x, g):
    return x * lax.rsqrt(jnp.mean(x * x, axis=-1, keepdims=True) + EPS) * g


def _dot(a, b):
    return jnp.dot(a, b, preferred_element_type=F32)


def _dot_nt(a, b):
    return lax.dot_general(a, b, (((1,), (1,)), ((), ())), preferred_element_type=F32)


def _split3(x):
    hi = x.astype(BF16)
    r = x - hi.astype(F32)
    mid = r.astype(BF16)
    lo = (r - mid.astype(F32)).astype(BF16)
    return hi, mid, lo


def _softplus2(z2):
    return jnp.maximum(z2, 0.0) + jnp.log(1.0 + jnp.exp2(-jnp.abs(z2))) * LOG2E


def _ffn_kernel(x_ref, g_ref, wg_ref, wu_ref, wd_ref, gn_ref, o_ref, on_ref, xn_ref):
    f = pl.program_id(1)

    @pl.when(f == 0)
    def _():
        xn_ref[...] = _rms(x_ref[...], g_ref[...]).astype(BF16)
        o_ref[...] = jnp.zeros_like(o_ref)

    xn = xn_ref[...]
    gate = _dot(xn, wg_ref[...])
    up = _dot(xn, wu_ref[...])
    h = (gate * jax.nn.sigmoid(gate) * up).astype(BF16)
    o_ref[...] += _dot(h, wd_ref[...])

    @pl.when(f == pl.num_programs(1) - 1)
    def _():
        y = x_ref[...] + MACARON * o_ref[...]
        o_ref[...] = y
        on_ref[...] = _rms(y, gn_ref[...]).astype(BF16)


def _ffn(x, g, wg, wu, wd, g_next, tm, tf):
    m, d = x.shape
    d_ff = wg.shape[1]
    row = lambda i, f: (i, 0)
    return pl.pallas_call(
        _ffn_kernel,
        grid=(m // tm, d_ff // tf),
        in_specs=[
            pl.BlockSpec((tm, d), row),
            pl.BlockSpec((1, d), lambda i, f: (0, 0)),
            pl.BlockSpec((d, tf), lambda i, f: (0, f)),
            pl.BlockSpec((d, tf), lambda i, f: (0, f)),
            pl.BlockSpec((tf, d), lambda i, f: (f, 0)),
            pl.BlockSpec((1, d), lambda i, f: (0, 0)),
        ],
        out_specs=[pl.BlockSpec((tm, d), row), pl.BlockSpec((tm, d), row)],
        out_shape=[jax.ShapeDtypeStruct((m, d), F32), jax.ShapeDtypeStruct((m, d), BF16)],
        scratch_shapes=[pltpu.VMEM((tm, d), BF16)],
        compiler_params=_params(("parallel", "arbitrary")),
        name="ffn",
    )(x, g, wg, wu, wd, g_next)


def _proj_kernel(xn_ref, w_ref, wf_ref, bf_ref, qkv_ref, ksb_ref, vsb_ref, kfx_ref, vfx_ref, lf_ref):
    n = pl.program_id(1)
    xn = xn_ref[...]
    res = _dot(xn, w_ref[...])
    is_q = jnp.logical_or(n == 0, n == 3)
    res_mx = res * jnp.where(is_q, SCALE * LOG2E, 1.0)
    for h in range(N_HEADS):
        qkv_ref[0, h] = res_mx[:, h * HEAD_DIM:(h + 1) * HEAD_DIM].astype(BF16)

    def store_heads(ref):
        for h in range(N_HEADS):
            ref[h] = res[:, h * HEAD_DIM:(h + 1) * HEAD_DIM]

    for idx, ref in ((1, ksb_ref), (2, vsb_ref), (4, kfx_ref), (5, vfx_ref)):
        pl.when(n == idx)(functools.partial(store_heads, ref))

    @pl.when(n == 0)
    def _():
        fz = _dot(xn, wf_ref[...]) + bf_ref[...]
        lf_ref[...] = jnp.minimum(fz, 0.0) - jnp.log1p(jnp.exp(-jnp.abs(fz)))


def _proj(xn, w_qkv, w_f, b_f, tm):
    m, d = xn.shape
    head_f32 = jax.ShapeDtypeStruct((N_HEADS, m, HEAD_DIM), F32)
    head_spec = pl.BlockSpec((N_HEADS, tm, HEAD_DIM), lambda i, n: (0, i, 0))
    return pl.pallas_call(
        _proj_kernel,
        grid=(m // tm, 6),
        in_specs=[
            pl.BlockSpec((tm, d), lambda i, n: (i, 0)),
            pl.BlockSpec((d, D_GROUP), lambda i, n: (0, n)),
            pl.BlockSpec((d, LANES), lambda i, n: (0, 0)),
            pl.BlockSpec((1, LANES), lambda i, n: (0, 0)),
        ],
        out_specs=[
            pl.BlockSpec((1, N_HEADS, tm, HEAD_DIM), lambda i, n: (n, 0, i, 0)),
            head_spec, head_spec, head_spec, head_spec,
            pl.BlockSpec((tm, LANES), lambda i, n: (i, 0)),
        ],
        out_shape=[
            jax.ShapeDtypeStruct((6, N_HEADS, m, HEAD_DIM), BF16),
            head_f32, head_f32, head_f32, head_f32,
            jax.ShapeDtypeStruct((m, LANES), F32),
        ],
        compiler_params=_params(("parallel", "arbitrary")),
        name="in_proj",
    )(xn, w_qkv, w_f, b_f)


def _cumsum_kernel(x_ref, o_ref):
    n_chunks, rows, _ = x_ref.shape
    upper = (lax.broadcasted_iota(jnp.int32, (LANES, LANES), 0)
             <= lax.broadcasted_iota(jnp.int32, (LANES, LANES), 1)).astype(BF16)

    def body(c, carry):
        hi, mid, lo = _split3(x_ref[c])
        cs = _dot(hi, upper) + _dot(mid, upper) + _dot(lo, upper) + carry
        o_ref[c] = cs * LOG2E
        return cs[:, LANES - 1:LANES]

    lax.fori_loop(0, n_chunks, body, jnp.zeros((rows, 1), F32))


def _cumsum_rows(x):
    rows, length = x.shape
    xc = x.reshape(rows, length // LANES, LANES).transpose(1, 0, 2)
    out = pl.pallas_call(
        _cumsum_kernel,
        out_shape=jax.ShapeDtypeStruct(xc.shape, F32),
        compiler_params=pltpu.CompilerParams(vmem_limit_bytes=V7X_VMEM_LIMIT),
        name="cumsum",
    )(xc)
    return out.transpose(1, 0, 2).reshape(rows, length)


def _suffix_ones(n):
    return (lax.broadcasted_iota(jnp.int32, (n, n), 0)
            >= lax.broadcasted_iota(jnp.int32, (n, n), 1)).astype(BF16)


def _sb_step(qs, k, v, suffix, states, masks):
    live = [c for c, m in enumerate(masks) if not isinstance(m, str)]
    z = {c: _dot_nt(qs[c], k) for c in live}
    sp = {}
    for c in live:
        s = _softplus2(z[c])
        sp[c] = s if masks[c] is None else jnp.where(masks[c], s, 0.0)
    incl = {c: _dot(sp[c].astype(BF16), suffix) for c in live}
    out = list(states)
    for c in live:
        carry, acc = states[c]
        a = jnp.exp2(z[c] - incl[c] - carry)
        if masks[c] is not None:
            a = jnp.where(masks[c], a, 0.0)
        out[c] = (carry + incl[c][:, :1], acc + _dot(a.astype(BF16), v))
    return out


def _fox_step(qs, k, v, biases, states, masks):
    live = [c for c, m in enumerate(masks) if not isinstance(m, str)]
    s = {}
    for c in live:
        sc = _dot_nt(qs[c], k) + biases[c]
        s[c] = sc if masks[c] is None else jnp.where(masks[c], sc, -jnp.inf)
    out = list(states)
    for c in live:
        m, l, acc = states[c]
        m_new = jnp.maximum(m, jnp.max(s[c], axis=-1, keepdims=True))
        alpha = jnp.exp2(m - m_new)
        p = jnp.exp2(s[c] - m_new)
        out[c] = (m_new, alpha * l + jnp.sum(p, axis=-1, keepdims=True), alpha * acc + _dot(p.astype(BF16), v))
    return out


def _diag_masks(jj, tk, rc, n_chains, strict):
    masks = []
    for c in range(n_chains):
        lo_row, hi_row = c * rc, (c + 1) * rc - 1
        lo_key, hi_key = jj * tk, (jj + 1) * tk - 1
        if (lo_key >= hi_row) if strict else (lo_key > hi_row):
            masks.append("skip")
        elif (hi_key < lo_row) if strict else (hi_key <= lo_row):
            masks.append(None)
        else:
            rows = lax.broadcasted_iota(jnp.int32, (rc, tk), 0) + lo_row
            cols = lax.broadcasted_iota(jnp.int32, (rc, tk), 1) + lo_key
            masks.append(cols < rows if strict else cols <= rows)
    return masks


def _head_norm(o, g):
    return (o * lax.rsqrt(jnp.mean(o * o, axis=-1, keepdims=True) + EPS) * g).astype(BF16)


def _sb_prompt_kernel(q_ref, k_ref, v_ref, g_ref, o_ref, z_ref, a_ref, acc_ref, *, tq, tk, n_chains):
    qi = pl.program_id(1)
    rc = tq // n_chains
    ratio = tq // tk
    qs = [q_ref[0, 0, c * rc:(c + 1) * rc, :] for c in range(n_chains)]
    suffix = _suffix_ones(tk)

    def kv(j):
        start = pl.multiple_of(j * tk, tk)
        return k_ref[0, 0, pl.ds(start, tk), :], v_ref[0, 0, pl.ds(start, tk), :]

    states = [(jnp.zeros((rc, 1), F32), jnp.zeros((rc, HEAD_DIM), F32)) for _ in range(n_chains)]
    for jj in reversed(range(ratio)):
        k, v = kv(qi * ratio + jj)
        states = _sb_step(qs, k, v, suffix, states, _diag_masks(jj, tk, rc, n_chains, True))

    n_main = qi * ratio
    chains = range(n_chains)

    def put_scores(slot, j):
        k, _ = kv(jnp.maximum(j, 0))
        for c in chains:
            z_ref[slot, c] = _dot_nt(qs[c], k)

    def half_trip(src, dst, j, j_prev, carries):
        put_scores(dst, j - 1)
        _, v_prev = kv(j_prev)
        for c in chains:
            acc_ref[c] += _dot(a_ref[src, c], v_prev)
        z = [z_ref[src, c] for c in chains]
        incl = [_dot(_softplus2(z[c]).astype(BF16), suffix) for c in chains]
        for c in chains:
            a_ref[dst, c] = jnp.exp2(z[c] - incl[c] - carries[c]).astype(BF16)
        return [carries[c] + incl[c][:, :1] for c in chains]

    def body(pair, loop_state):
        j_prev, carries = loop_state
        j = n_main - 1 - 2 * pair
        carries = half_trip(0, 1, j, j_prev, carries)
        carries = half_trip(1, 0, j - 1, j, carries)
        return j - 1, carries

    put_scores(0, n_main - 1)
    for c in chains:
        a_ref[0, c] = jnp.zeros((rc, tk), BF16)
        acc_ref[c] = states[c][1]
    j_last, _ = lax.fori_loop(0, n_main // 2, body, (0, [states[c][0] for c in chains]))
    _, v_last = kv(j_last)
    for c in chains:
        o_ref[c * rc:(c + 1) * rc, :] = _head_norm(acc_ref[c] + _dot(a_ref[0, c], v_last), g_ref[0])


def _fox_prompt_kernel(q_ref, k_ref, v_ref, cq_ref, ck_ref, g_ref, o_ref, z_ref, p_ref, acc_ref, *, tq, tk, n_chains):
    qi = pl.program_id(1)
    rc = tq // n_chains
    ratio = tq // tk
    qs = [q_ref[0, 0, c * rc:(c + 1) * rc, :] for c in range(n_chains)]
    cqs = [cq_ref[0, c * rc:(c + 1) * rc, :] for c in range(n_chains)]

    def kv(j):
        start = pl.multiple_of(j * tk, tk)
        return k_ref[0, 0, pl.ds(start, tk), :], v_ref[0, 0, pl.ds(start, tk), :]

    lane = lax.broadcasted_iota(jnp.int32, (1, HEAD_DIM), 1)
    ones_col = jnp.broadcast_to(jnp.where(lane == 0, 1.0, 0.0).astype(BF16), (tk, HEAD_DIM))

    def values(j):
        _, v = kv(j)
        return jnp.concatenate([v, ones_col], axis=1)

    masks = [_diag_masks(jj, tk, rc, n_chains, False) for jj in range(ratio)]
    states = []
    for c in range(n_chains):
        seen = [jj for jj in range(ratio) if not isinstance(masks[jj][c], str)]
        s = []
        for jj in seen:
            sc = _dot_nt(qs[c], kv(qi * ratio + jj)[0]) + (cqs[c] - ck_ref[0, qi * ratio + jj])
            s.append(sc if masks[jj][c] is None else jnp.where(masks[jj][c], sc, -jnp.inf))
        m = functools.reduce(jnp.maximum, [jnp.max(x, axis=-1, keepdims=True) for x in s])
        acc = functools.reduce(jnp.add, [_dot(jnp.exp2(x - m).astype(BF16), values(qi * ratio + jj))
                                         for x, jj in zip(s, seen)])
        states.append((m, acc))

    n_main = qi * ratio
    chains = range(n_chains)

    def put_scores(slot, j):
        k, _ = kv(jnp.maximum(j, 0))
        for c in chains:
            z_ref[slot, c] = _dot_nt(qs[c], k)

    def half_trip(src, dst, j, j_prev, maxes):
        put_scores(dst, j - 1)
        v_prev = values(j_prev)
        ck = ck_ref[0, j]
        out = []
        for c in chains:
            s = z_ref[src, c] + (cqs[c] - ck)
            m_new = jnp.maximum(maxes[c], jnp.max(s, axis=-1, keepdims=True))
            acc_ref[c] = (acc_ref[c] + _dot(p_ref[src, c], v_prev)) * jnp.exp2(maxes[c] - m_new)
            p_ref[dst, c] = jnp.exp2(s - m_new).astype(BF16)
            out.append(m_new)
        return out

    def body(pair, loop_state):
        j_prev, maxes = loop_state
        j = n_main - 1 - 2 * pair
        maxes = half_trip(0, 1, j, j_prev, maxes)
        maxes = half_trip(1, 0, j - 1, j, maxes)
        return j - 1, maxes

    put_scores(0, n_main - 1)
    for c in chains:
        p_ref[0, c] = jnp.zeros((rc, tk), BF16)
        acc_ref[c] = states[c][1]
    j_last, _ = lax.fori_loop(0, n_main // 2, body, (0, [states[c][0] for c in chains]))
    v_last = values(j_last)
    for c in chains:
        acc = acc_ref[c] + _dot(p_ref[0, c], v_last)
        o_ref[c * rc:(c + 1) * rc, :] = _head_norm(acc[:, :HEAD_DIM] / acc[:, HEAD_DIM:HEAD_DIM + 1], g_ref[0])


SB_TILES = (2048, 256, 8)
FOX_TILES = (2048, 512, 4)


def _attn_call(kernel_fn, name, tiles, t, acc_lanes, extra_specs):
    tq, tk, n_chains = (min(tiles[0], t), min(tiles[1], t), tiles[2])
    rc = tq // n_chains
    assert (tq // tk) % 2 == 0, "the pipelined sweep walks key blocks in pairs"
    first = 0 if name == "sb_prompt" else 3
    q_spec = pl.BlockSpec((1, 1, tq, HEAD_DIM), lambda h, i: (first, h, i, 0))
    kv_spec = lambda n: pl.BlockSpec((1, 1, t, HEAD_DIM), lambda h, i: (n, h, 0, 0))
    g_spec = pl.BlockSpec((1, 1, HEAD_DIM), lambda h, i: (h, 0, 0))
    return pl.pallas_call(
        functools.partial(kernel_fn, tq=tq, tk=tk, n_chains=n_chains),
        grid=(N_HEADS, t // tq),
        in_specs=[q_spec, kv_spec(first + 1), kv_spec(first + 2)] + extra_specs(tq, tk) + [g_spec],
        out_specs=pl.BlockSpec((tq, HEAD_DIM), lambda h, i: (i, h)),
        out_shape=jax.ShapeDtypeStruct((t, D_GROUP), BF16),
        scratch_shapes=[pltpu.VMEM((2, n_chains, rc, tk), F32), pltpu.VMEM((2, n_chains, rc, tk), BF16),
                        pltpu.VMEM((n_chains, rc, acc_lanes), F32)],
        compiler_params=_params(("parallel", "arbitrary")),
        name=name,
    )


def _prompt_attention(qkv, c_rows, g_sb, g_fx):
    t = qkv.shape[2]
    o_sb = _attn_call(_sb_prompt_kernel, "sb_prompt", SB_TILES, t, HEAD_DIM, lambda tq, tk: [])(qkv, qkv, qkv, g_sb)
    fox_tk = min(FOX_TILES[1], t)
    c_specs = lambda tq, tk: [pl.BlockSpec((1, tq, 1), lambda h, i: (h, i, 0)),
                              pl.BlockSpec((1, t // tk, 1, tk), lambda h, i: (h, 0, 0, 0))]
    o_fx = _attn_call(_fox_prompt_kernel, "fox_prompt", FOX_TILES, t, 2 * HEAD_DIM, c_specs)(
        qkv, qkv, qkv, c_rows.reshape(N_HEADS, t, 1), c_rows.reshape(N_HEADS, t // fox_tk, 1, fox_tk), g_fx)
    return o_sb, o_fx


def _sb_sample_kernel(q_ref, k_ref, v_ref, pk_ref, pv_ref, g_ref, o_ref, *, blk):
    tq = q_ref.shape[3]
    past = pk_ref.shape[3]
    suffix_new, suffix = _suffix_ones(tq), _suffix_ones(blk)
    blocks = range(past // blk)
    for hh in range(q_ref.shape[1]):
        q = q_ref[0, hh, 0]
        states = [(jnp.zeros((tq, 1), F32), jnp.zeros((tq, HEAD_DIM), F32))]
        (carry, acc), = _sb_step([q], k_ref[0, hh, 0], v_ref[0, hh, 0], suffix_new, states,
                                 _diag_masks(0, tq, tq, 1, True))
        k_blk = lambda j: pk_ref[0, 0, hh, j * blk:(j + 1) * blk, :].astype(BF16)
        v_blk = lambda j: pv_ref[0, 0, hh, j * blk:(j + 1) * blk, :].astype(BF16)
        z = [_dot_nt(q, k_blk(j)) for j in blocks]
        incl = [_dot(_softplus2(z[j]).astype(BF16), suffix) for j in blocks]
        for j in reversed(blocks):
            acc = acc + _dot(jnp.exp2(z[j] - incl[j] - carry).astype(BF16), v_blk(j))
            carry = carry + incl[j][:, :1]
        o_ref[:, hh * HEAD_DIM:(hh + 1) * HEAD_DIM] = _head_norm(acc, g_ref[hh])


def _fox_sample_kernel(q_ref, k_ref, v_ref, pk_ref, pv_ref, cq_ref, cn_ref, cp_ref, g_ref, o_ref, *, blk):
    tq = q_ref.shape[3]
    past = pk_ref.shape[3]
    blocks = range(past // blk)
    (mask,) = _diag_masks(0, tq, tq, 1, False)
    for hh in range(q_ref.shape[1]):
        q = q_ref[0, hh, 0]
        cq = cq_ref[0, hh]
        k_blk = lambda j: pk_ref[0, 0, hh, j * blk:(j + 1) * blk, :].astype(BF16)
        v_blk = lambda j: pv_ref[0, 0, hh, j * blk:(j + 1) * blk, :].astype(BF16)
        s_new = jnp.where(mask, _dot_nt(q, k_ref[0, hh, 0]) + (cq - cn_ref[0, hh]), -jnp.inf)
        s = [_dot_nt(q, k_blk(j)) + (cq - cp_ref[0, hh, j]) for j in blocks]
        m = jnp.max(s_new, axis=-1, keepdims=True)
        for j in blocks:
            m = jnp.maximum(m, jnp.max(s[j], axis=-1, keepdims=True))
        p_new = jnp.exp2(s_new - m)
        l = jnp.sum(p_new, axis=-1, keepdims=True)
        acc = _dot(p_new.astype(BF16), v_ref[0, hh, 0])
        for j in blocks:
            p = jnp.exp2(s[j] - m)
            l = l + jnp.sum(p, axis=-1, keepdims=True)
            acc = acc + _dot(p.astype(BF16), v_blk(j))
        o_ref[:, hh * HEAD_DIM:(hh + 1) * HEAD_DIM] = _head_norm(acc / l, g_ref[hh])


SAMPLE_HEADS_PER_STEP = 4


def _sample_attention(qkv, past_sb_k, past_sb_v, past_fx_k, past_fx_v, c_all, g_sb, g_fx, blk):
    b, _, past, _ = past_sb_k.shape
    tq = qkv.shape[2] // b
    hp = SAMPLE_HEADS_PER_STEP
    grid = (b, N_HEADS // hp)
    qkv = qkv.reshape(6, N_HEADS, b, tq, HEAD_DIM)
    new_spec = lambda n: pl.BlockSpec((1, hp, 1, tq, HEAD_DIM), lambda bi, h: (n, h, bi, 0, 0))
    past_spec = pl.BlockSpec((1, 1, hp, past, HEAD_DIM), lambda bi, h: (0, bi, h, 0, 0))
    g_spec = pl.BlockSpec((hp, 1, HEAD_DIM), lambda bi, h: (h, 0, 0))
    o_spec = pl.BlockSpec((tq, hp * HEAD_DIM), lambda bi, h: (bi, h))
    o_shape = jax.ShapeDtypeStruct((b * tq, D_GROUP), BF16)
    o_sb = pl.pallas_call(
        functools.partial(_sb_sample_kernel, blk=blk),
        grid=grid,
        in_specs=[new_spec(0), new_spec(1), new_spec(2), past_spec, past_spec, g_spec],
        out_specs=o_spec, out_shape=o_shape,
        compiler_params=_params(("parallel", "parallel")),
        name="sb_sample",
    )(qkv, qkv, qkv, past_sb_k[None], past_sb_v[None], g_sb)
    n_blk = past // blk
    cq = c_all[:, :, past:].reshape(b, N_HEADS, tq, 1)
    cn = c_all[:, :, past:].reshape(b, N_HEADS, 1, tq)
    cp = c_all[:, :, :past].reshape(b, N_HEADS, n_blk, 1, blk)
    o_fx = pl.pallas_call(
        functools.partial(_fox_sample_kernel, blk=blk),
        grid=grid,
        in_specs=[new_spec(3), new_spec(4), new_spec(5), past_spec, past_spec,
                  pl.BlockSpec((1, hp, tq, 1), lambda bi, h: (bi, h, 0, 0)),
                  pl.BlockSpec((1, hp, 1, tq), lambda bi, h: (bi, h, 0, 0)),
                  pl.BlockSpec((1, hp, n_blk, 1, blk), lambda bi, h: (bi, h, 0, 0, 0)),
                  g_spec],
        out_specs=o_spec, out_shape=o_shape,
        compiler_params=_params(("parallel", "parallel")),
        name="fox_sample",
    )(qkv, qkv, qkv, past_fx_k[None], past_fx_v[None], cq, cn, cp, g_fx)
    return o_sb, o_fx


def _out_proj_kernel(x_ref, osb_ref, ofx_ref, wsb_ref, wfx_ref, o_ref):
    o_ref[...] = x_ref[...] + _dot(osb_ref[...], wsb_ref[...]) + _dot(ofx_ref[...], wfx_ref[...])


def _out_proj(x, o_sb, o_fx, w_out, tm):
    m, d = x.shape
    row = lambda i: (i, 0)
    return pl.pallas_call(
        _out_proj_kernel,
        grid=(m // tm,),
        in_specs=[pl.BlockSpec((tm, d), row), pl.BlockSpec((tm, D_GROUP), row), pl.BlockSpec((tm, D_GROUP), row),
                  pl.BlockSpec((D_GROUP, d), lambda i: (0, 0)), pl.BlockSpec((D_GROUP, d), lambda i: (1, 0))],
        out_specs=pl.BlockSpec((tm, d), row),
        out_shape=jax.ShapeDtypeStruct((m, d), F32),
        compiler_params=_params(("parallel",)),
        name="out_proj",
    )(x, o_sb, o_fx, w_out, w_out)


def _ple_kernel(x_ref, xn_ref, p_ref, wg_ref, wp_ref, gf_ref, o_ref):
    gate = jax.nn.sigmoid(_dot(xn_ref[...], wg_ref[...]))
    emb = _dot(p_ref[...].astype(BF16), wp_ref[...])
    o_ref[...] = _rms(x_ref[...] + gate * emb, gf_ref[...])


def _ple_final(x, xn, p, w_gate, w_proj, g_final, tm):
    m, d = x.shape
    ple = p.shape[1]
    row = lambda i: (i, 0)
    full = lambda i: (0, 0)
    return pl.pallas_call(
        _ple_kernel,
        grid=(m // tm,),
        in_specs=[pl.BlockSpec((tm, d), row), pl.BlockSpec((tm, d), row), pl.BlockSpec((tm, ple), row),
                  pl.BlockSpec((d, d), full), pl.BlockSpec((ple, d), full), pl.BlockSpec((1, d), full)],
        out_specs=pl.BlockSpec((tm, d), row),
        out_shape=jax.ShapeDtypeStruct((m, d), F32),
        compiler_params=_params(("parallel",)),
        name="ple_final",
    )(x, xn, p, w_gate, w_proj, g_final)


def _pad_lanes(x, n):
    return jnp.pad(x, ((0, 0), (0, n - x.shape[1])))


def _run_group(x, p, past, w, attn_blk):
    b, t, d = x.shape
    m = b * t
    tm = min(512, m)
    row = lambda g: g.reshape(1, -1)
    x0 = x.reshape(m, d)
    x1, xn = _ffn(x0, row(w["g_ffn1"]), w["ffn1_gate"], w["ffn1_up"], w["ffn1_down"], row(w["g_mix"]), tm, 512)
    qkv, k_sb, v_sb, k_fx, v_fx, lf_cols = _proj(xn, w["in_qkv"], w["in_f"], w["b_f"], tm)
    lf = lf_cols[:, :N_HEADS].reshape(b, t, N_HEADS).transpose(0, 2, 1)
    g_sb = w["g_out_sb"].reshape(N_HEADS, 1, HEAD_DIM)
    g_fx = w["g_out_fox"].reshape(N_HEADS, 1, HEAD_DIM)
    if past is None:
        c_rows = _cumsum_rows(lf.reshape(N_HEADS, t))
        o_sb, o_fx = _prompt_attention(qkv, c_rows, g_sb, g_fx)
    else:
        lf_all = jnp.concatenate([past[4], lf], axis=-1).reshape(b * N_HEADS, -1)
        total = lf_all.shape[1]
        padded = -(-total // LANES) * LANES
        c_all = _cumsum_rows(_pad_lanes(lf_all, padded))[:, :total].reshape(b, N_HEADS, total)
        o_sb, o_fx = _sample_attention(qkv, past[0], past[1], past[2], past[3], c_all, g_sb, g_fx, attn_blk)
    x2 = _out_proj(x1, o_sb, o_fx, w["out"], tm)
    x3, xn3 = _ffn(x2, row(w["g_ffn2"]), w["ffn2_gate"], w["ffn2_up"], w["ffn2_down"], row(w["g_ple"]), tm, 512)
    y = _ple_final(x3, xn3, p.reshape(m, -1), w["ple_gate"], w["ple_proj"], row(w["g_final"]), min(256, m))
    heads = lambda a: a.reshape(N_HEADS, b, t, HEAD_DIM).transpose(1, 0, 2, 3)[None]
    return y.reshape(b, t, d), heads(k_sb), heads(v_sb), heads(k_fx), heads(v_fx), lf[None]


def kernel(x_prompt, x_sample, cache_sb_k, cache_sb_v, cache_fox_k, cache_fox_v, cache_fox_logf, p_prompt, p_sample, g_ffn1, w_ffn1_gate, w_ffn1_up, w_ffn1_down, g_mix, w_in, b_forget, g_out_sb, g_out_fox, w_out, g_ffn2, w_ffn2_gate, w_ffn2_up, w_ffn2_down, g_ple, w_ple_gate, w_ple_proj, g_final):
    assert g_ffn1.shape[0] == 1, "single-layer trunk"
    bf = lambda a: a.astype(BF16)
    n_qkv = 6 * D_GROUP
    w = {
        "g_ffn1": g_ffn1[0], "ffn1_gate": bf(w_ffn1_gate[0]), "ffn1_up": bf(w_ffn1_up[0]), "ffn1_down": bf(w_ffn1_down[0]),
        "g_mix": g_mix[0], "in_qkv": bf(w_in[0]),
        "in_f": _pad_lanes(bf(w_in[0][:, n_qkv:]), LANES), "b_f": _pad_lanes(b_forget[0][None, :], LANES),
        "g_out_sb": g_out_sb[0], "g_out_fox": g_out_fox[0],
        "out": bf(w_out[0]),
        "g_ffn2": g_ffn2[0], "ffn2_gate": bf(w_ffn2_gate[0]), "ffn2_up": bf(w_ffn2_up[0]), "ffn2_down": bf(w_ffn2_down[0]),
        "g_ple": g_ple[0], "ple_gate": bf(w_ple_gate[0]), "ple_proj": bf(w_ple_proj[0]), "g_final": g_final,
    }
    past = (cache_sb_k[0], cache_sb_v[0], cache_fox_k[0], cache_fox_v[0], cache_fox_logf[0])
    y_p, sb_k_p, sb_v_p, fx_k_p, fx_v_p, lf_p = _run_group(x_prompt, p_prompt[0], None, w, 256)
    y_s, sb_k_s, sb_v_s, fx_k_s, fx_v_s, lf_s = _run_group(x_sample, p_sample[0], past, w, 256)
    return (y_p, y_s, sb_k_p, sb_v_p, fx_k_p, fx_v_p, lf_p, sb_k_s, sb_v_s, fx_k_s, fx_v_s, lf_s)
```

```python
import functools
import math

import jax
import jax.numpy as jnp
from jax import lax
from jax.experimental import pallas as pl
from jax.experimental.pallas import tpu as pltpu

HEAD_DIM = 128
N_HEADS = 8
D_GROUP = N_HEADS * HEAD_DIM
MACARON = 0.5
EPS = 1e-6
SCALE = 1.0 / math.sqrt(HEAD_DIM)
LOG2E = math.log2(math.e)
LANES = 128
V7X_VMEM_LIMIT = 56 * 1024 * 1024

F32 = jnp.float32
BF16 = jnp.bfloat16


def _params(semantics):
    return pltpu.CompilerParams(dimension_semantics=semantics, vmem_limit_bytes=V7X_VMEM_LIMIT)


def _rms(x, g):
    return x * lax.rsqrt(jnp.mean(x * x, axis=-1, keepdims=True) + EPS) * g


def _dot(a, b):
    return jnp.dot(a, b, preferred_element_type=F32)


def _dot_nt(a, b):
    return lax.dot_general(a, b, (((1,), (1,)), ((), ())), preferred_element_type=F32)


def _split3(x):
    hi = x.astype(BF16)
    r = x - hi.astype(F32)
    mid = r.astype(BF16)
    lo = (r - mid.astype(F32)).astype(BF16)
    return hi, mid, lo


def _softplus2(z2):
    return jnp.maximum(z2, 0.0) + jnp.log(1.0 + jnp.exp2(-jnp.abs(z2))) * LOG2E


def _ffn_kernel(x_ref, g_ref, wg_ref, wu_ref, wd_ref, gn_ref, o_ref, on_ref, xn_ref):
    f = pl.program_id(1)

    @pl.when(f == 0)
    def _():
        xn_ref[...] = _rms(x_ref[...], g_ref[...]).astype(BF16)
        o_ref[...] = jnp.zeros_like(o_ref)

    xn = xn_ref[...]
    gate = _dot(xn, wg_ref[...])
    up = _dot(xn, wu_ref[...])
    h = (gate * jax.nn.sigmoid(gate) * up).astype(BF16)
    o_ref[...] += _dot(h, wd_ref[...])

    @pl.when(f == pl.num_programs(1) - 1)
    def _():
        y = x_ref[...] + MACARON * o_ref[...]
        o_ref[...] = y
        on_ref[...] = _rms(y, gn_ref[...]).astype(BF16)


def _ffn(x, g, wg, wu, wd, g_next, tm, tf):
    m, d = x.shape
    d_ff = wg.shape[1]
    row = lambda i, f: (i, 0)
    return pl.pallas_call(
        _ffn_kernel,
        grid=(m // tm, d_ff // tf),
        in_specs=[
            pl.BlockSpec((tm, d), row),
            pl.BlockSpec((1, d), lambda i, f: (0, 0)),
            pl.BlockSpec((d, tf), lambda i, f: (0, f)),
            pl.BlockSpec((d, tf), lambda i, f: (0, f)),
            pl.BlockSpec((tf, d), lambda i, f: (f, 0)),
            pl.BlockSpec((1, d), lambda i, f: (0, 0)),
        ],
        out_specs=[pl.BlockSpec((tm, d), row), pl.BlockSpec((tm, d), row)],
        out_shape=[jax.ShapeDtypeStruct((m, d), F32), jax.ShapeDtypeStruct((m, d), BF16)],
        scratch_shapes=[pltpu.VMEM((tm, d), BF16)],
        compiler_params=_params(("parallel", "arbitrary")),
        name="ffn",
    )(x, g, wg, wu, wd, g_next)


def _proj_kernel(xn_ref, w_ref, wf_ref, bf_ref, qkv_ref, ksb_ref, vsb_ref, kfx_ref, vfx_ref, lf_ref):
    n = pl.program_id(1)
    xn = xn_ref[...]
    res = _dot(xn, w_ref[...])
    is_q = jnp.logical_or(n == 0, n == 3)
    res_mx = res * jnp.where(is_q, SCALE * LOG2E, 1.0)
    for h in range(N_HEADS):
        qkv_ref[0, h] = res_mx[:, h * HEAD_DIM:(h + 1) * HEAD_DIM].astype(BF16)

    def store_heads(ref):
        for h in range(N_HEADS):
            ref[h] = res[:, h * HEAD_DIM:(h + 1) * HEAD_DIM]

    for idx, ref in ((1, ksb_ref), (2, vsb_ref), (4, kfx_ref), (5, vfx_ref)):
        pl.when(n == idx)(functools.partial(store_heads, ref))

    @pl.when(n == 0)
    def _():
        fz = _dot(xn, wf_ref[...]) + bf_ref[...]
        lf_ref[...] = jnp.minimum(fz, 0.0) - jnp.log1p(jnp.exp(-jnp.abs(fz)))


def _proj(xn, w_qkv, w_f, b_f, tm):
    m, d = xn.shape
    head_f32 = jax.ShapeDtypeStruct((N_HEADS, m, HEAD_DIM), F32)
    head_spec = pl.BlockSpec((N_HEADS, tm, HEAD_DIM), lambda i, n: (0, i, 0))
    return pl.pallas_call(
        _proj_kernel,
        grid=(m // tm, 6),
        in_specs=[
            pl.BlockSpec((tm, d), lambda i, n: (i, 0)),
            pl.BlockSpec((d, D_GROUP), lambda i, n: (0, n)),
            pl.BlockSpec((d, LANES), lambda i, n: (0, 0)),
            pl.BlockSpec((1, LANES), lambda i, n: (0, 0)),
        ],
        out_specs=[
            pl.BlockSpec((1, N_HEADS, tm, HEAD_DIM), lambda i, n: (n, 0, i, 0)),
            head_spec, head_spec, head_spec, head_spec,
            pl.BlockSpec((tm, LANES), lambda i, n: (i, 0)),
        ],
        out_shape=[
            jax.ShapeDtypeStruct((6, N_HEADS, m, HEAD_DIM), BF16),
            head_f32, head_f32, head_f32, head_f32,
            jax.ShapeDtypeStruct((m, LANES), F32),
        ],
        compiler_params=_params(("parallel", "arbitrary")),
        name="in_proj",
    )(xn, w_qkv, w_f, b_f)


def _cumsum_kernel(x_ref, o_ref):
    n_chunks, rows, _ = x_ref.shape
    upper = (lax.broadcasted_iota(jnp.int32, (LANES, LANES), 0)
             <= lax.broadcasted_iota(jnp.int32, (LANES, LANES), 1)).astype(BF16)

    def body(c, carry):
        hi, mid, lo = _split3(x_ref[c])
        cs = _dot(hi, upper) + _dot(mid, upper) + _dot(lo, upper) + carry
        o_ref[c] = cs * LOG2E
        return cs[:, LANES - 1:LANES]

    lax.fori_loop(0, n_chunks, body, jnp.zeros((rows, 1), F32))


def _cumsum_rows(x):
    rows, length = x.shape
    xc = x.reshape(rows, length // LANES, LANES).transpose(1, 0, 2)
    out = pl.pallas_call(
        _cumsum_kernel,
        out_shape=jax.ShapeDtypeStruct(xc.shape, F32),
        compiler_params=pltpu.CompilerParams(vmem_limit_bytes=V7X_VMEM_LIMIT),
        name="cumsum",
    )(xc)
    return out.transpose(1, 0, 2).reshape(rows, length)


def _suffix_ones(n):
    return (lax.broadcasted_iota(jnp.int32, (n, n), 0)
            >= lax.broadcasted_iota(jnp.int32, (n, n), 1)).astype(BF16)


def _sb_step(qs, k, v, suffix, states, masks):
    live = [c for c, m in enumerate(masks) if not isinstance(m, str)]
    z = {c: _dot_nt(qs[c], k) for c in live}
    sp = {}
    for c in live:
        s = _softplus2(z[c])
        sp[c] = s if masks[c] is None else jnp.where(masks[c], s, 0.0)
    incl = {c: _dot(sp[c].astype(BF16), suffix) for c in live}
    out = list(states)
    for c in live:
        carry, acc = states[c]
        a = jnp.exp2(z[c] - incl[c] - carry)
        if masks[c] is not None:
            a = jnp.where(masks[c], a, 0.0)
        out[c] = (carry + incl[c][:, :1], acc + _dot(a.astype(BF16), v))
    return out


def _fox_step(qs, k, v, biases, states, masks):
    live = [c for c, m in enumerate(masks) if not isinstance(m, str)]
    s = {}
    for c in live:
        sc = _dot_nt(qs[c], k) + biases[c]
        s[c] = sc if masks[c] is None else jnp.where(masks[c], sc, -jnp.inf)
    out = list(states)
    for c in live:
        m, l, acc = states[c]
        m_new = jnp.maximum(m, jnp.max(s[c], axis=-1, keepdims=True))
        alpha = jnp.exp2(m - m_new)
        p = jnp.exp2(s[c] - m_new)
        out[c] = (m_new, alpha * l + jnp.sum(p, axis=-1, keepdims=True), alpha * acc + _dot(p.astype(BF16), v))
    return out


def _diag_masks(jj, tk, rc, n_chains, strict):
    masks = []
    for c in range(n_chains):
        lo_row, hi_row = c * rc, (c + 1) * rc - 1
        lo_key, hi_key = jj * tk, (jj + 1) * tk - 1
        if (lo_key >= hi_row) if strict else (lo_key > hi_row):
            masks.append("skip")
        elif (hi_key < lo_row) if strict else (hi_key <= lo_row):
            masks.append(None)
        else:
            rows = lax.broadcasted_iota(jnp.int32, (rc, tk), 0) + lo_row
            cols = lax.broadcasted_iota(jnp.int32, (rc, tk), 1) + lo_key
            masks.append(cols < rows if strict else cols <= rows)
    return masks


def _head_norm(o, g):
    return (o * lax.rsqrt(jnp.mean(o * o, axis=-1, keepdims=True) + EPS) * g).astype(BF16)


def _sb_prompt_kernel(q_ref, k_ref, v_ref, g_ref, o_ref, z_ref, a_ref, acc_ref, *, tq, tk, n_chains):
    qi = pl.program_id(1)
    rc = tq // n_chains
    ratio = tq // tk
    qs = [q_ref[0, 0, c * rc:(c + 1) * rc, :] for c in range(n_chains)]
    suffix = _suffix_ones(tk)

    def kv(j):
        start = pl.multiple_of(j * tk, tk)
        return k_ref[0, 0, pl.ds(start, tk), :], v_ref[0, 0, pl.ds(start, tk), :]

    states = [(jnp.zeros((rc, 1), F32), jnp.zeros((rc, HEAD_DIM), F32)) for _ in range(n_chains)]
    for jj in reversed(range(ratio)):
        k, v = kv(qi * ratio + jj)
        states = _sb_step(qs, k, v, suffix, states, _diag_masks(jj, tk, rc, n_chains, True))

    n_main = qi * ratio
    chains = range(n_chains)

    def put_scores(slot, j):
        k, _ = kv(jnp.maximum(j, 0))
        for c in chains:
            z_ref[slot, c] = _dot_nt(qs[c], k)

    def half_trip(src, dst, j, j_prev, carries):
        put_scores(dst, j - 1)
        _, v_prev = kv(j_prev)
        for c in chains:
            acc_ref[c] += _dot(a_ref[src, c], v_prev)
        z = [z_ref[src, c] for c in chains]
        incl = [_dot(_softplus2(z[c]).astype(BF16), suffix) for c in chains]
        for c in chains:
            a_ref[dst, c] = jnp.exp2(z[c] - incl[c] - carries[c]).astype(BF16)
        return [carries[c] + incl[c][:, :1] for c in chains]

    def body(pair, loop_state):
        j_prev, carries = loop_state
        j = n_main - 1 - 2 * pair
        carries = half_trip(0, 1, j, j_prev, carries)
        carries = half_trip(1, 0, j - 1, j, carries)
        return j - 1, carries

    put_scores(0, n_main - 1)
    for c in chains:
        a_ref[0, c] = jnp.zeros((rc, tk), BF16)
        acc_ref[c] = states[c][1]
    j_last, _ = lax.fori_loop(0, n_main // 2, body, (0, [states[c][0] for c in chains]))
    _, v_last = kv(j_last)
    for c in chains:
        o_ref[c * rc:(c + 1) * rc, :] = _head_norm(acc_ref[c] + _dot(a_ref[0, c], v_last), g_ref[0])


def _fox_prompt_kernel(q_ref, k_ref, v_ref, cq_ref, ck_ref, g_ref, o_ref, z_ref, p_ref, acc_ref, *, tq, tk, n_chains):
    qi = pl.program_id(1)
    rc = tq // n_chains
    ratio = tq // tk
    qs = [q_ref[0, 0, c * rc:(c + 1) * rc, :] for c in range(n_chains)]
    cqs = [cq_ref[0, c * rc:(c + 1) * rc, :] for c in range(n_chains)]

    def kv(j):
        start = pl.multiple_of(j * tk, tk)
        return k_ref[0, 0, pl.ds(start, tk), :], v_ref[0, 0, pl.ds(start, tk), :]

    lane = lax.broadcasted_iota(jnp.int32, (1, HEAD_DIM), 1)
    ones_col = jnp.broadcast_to(jnp.where(lane == 0, 1.0, 0.0).astype(BF16), (tk, HEAD_DIM))

    def values(j):
        _, v = kv(j)
        return jnp.concatenate([v, ones_col], axis=1)

    masks = [_diag_masks(jj, tk, rc, n_chains, False) for jj in range(ratio)]
    states = []
    for c in range(n_chains):
        seen = [jj for jj in range(ratio) if not isinstance(masks[jj][c], str)]
        s = []
        for jj in seen:
            sc = _dot_nt(qs[c], kv(qi * ratio + jj)[0]) + (cqs[c] - ck_ref[0, qi * ratio + jj])
            s.append(sc if masks[jj][c] is None else jnp.where(masks[jj][c], sc, -jnp.inf))
        m = functools.reduce(jnp.maximum, [jnp.max(x, axis=-1, keepdims=True) for x in s])
        acc = functools.reduce(jnp.add, [_dot(jnp.exp2(x - m).astype(BF16), values(qi * ratio + jj))
                                         for x, jj in zip(s, seen)])
        states.append((m, acc))

    n_main = qi * ratio
    chains = range(n_chains)

    def put_scores(slot, j):
        k, _ = kv(jnp.maximum(j, 0))
        for c in chains:
            z_ref[slot, c] = _dot_nt(qs[c], k)

    def half_trip(src, dst, j, j_prev, maxes):
        put_scores(dst, j - 1)
        v_prev = values(j_prev)
        ck = ck_ref[0, j]
        out = []
        for c in chains:
            s = z_ref[src, c] + (cqs[c] - ck)
            m_new = jnp.maximum(maxes[c], jnp.max(s, axis=-1, keepdims=True))
            acc_ref[c] = (acc_ref[c] + _dot(p_ref[src, c], v_prev)) * jnp.exp2(maxes[c] - m_new)
            p_ref[dst, c] = jnp.exp2(s - m_new).astype(BF16)
            out.append(m_new)
        return out

    def body(pair, loop_state):
        j_prev, maxes = loop_state
        j = n_main - 1 - 2 * pair
        maxes = half_trip(0, 1, j, j_prev, maxes)
        maxes = half_trip(1, 0, j - 1, j, maxes)
        return j - 1, maxes

    put_scores(0, n_main - 1)
    for c in chains:
        p_ref[0, c] = jnp.zeros((rc, tk), BF16)
        acc_ref[c] = states[c][1]
    j_last, _ = lax.fori_loop(0, n_main // 2, body, (0, [states[c][0] for c in chains]))
    v_last = values(j_last)
    for c in chains:
        acc = acc_ref[c] + _dot(p_ref[0, c], v_last)
        o_ref[c * rc:(c + 1) * rc, :] = _head_norm(acc[:, :HEAD_DIM] / acc[:, HEAD_DIM:HEAD_DIM + 1], g_ref[0])


SB_TILES = (2048, 256, 8)
FOX_TILES = (2048, 512, 4)


def _attn_call(kernel_fn, name, tiles, t, acc_lanes, extra_specs):
    tq, tk, n_chains = (min(tiles[0], t), min(tiles[1], t), tiles[2])
    rc = tq // n_chains
    assert (tq // tk) % 2 == 0, "the pipelined sweep walks key blocks in pairs"
    first = 0 if name == "sb_prompt" else 3
    q_spec = pl.BlockSpec((1, 1, tq, HEAD_DIM), lambda h, i: (first, h, i, 0))
    kv_spec = lambda n: pl.BlockSpec((1, 1, t, HEAD_DIM), lambda h, i: (n, h, 0, 0))
    g_spec = pl.BlockSpec((1, 1, HEAD_DIM), lambda h, i: (h, 0, 0))
    return pl.pallas_call(
        functools.partial(kernel_fn, tq=tq, tk=tk, n_chains=n_chains),
        grid=(N_HEADS, t // tq),
        in_specs=[q_spec, kv_spec(first + 1), kv_spec(first + 2)] + extra_specs(tq, tk) + [g_spec],
        out_specs=pl.BlockSpec((tq, HEAD_DIM), lambda h, i: (i, h)),
        out_shape=jax.ShapeDtypeStruct((t, D_GROUP), BF16),
        scratch_shapes=[pltpu.VMEM((2, n_chains, rc, tk), F32), pltpu.VMEM((2, n_chains, rc, tk), BF16),
                        pltpu.VMEM((n_chains, rc, acc_lanes), F32)],
        compiler_params=_params(("parallel", "arbitrary")),
        name=name,
    )


def _prompt_attention(qkv, c_rows, g_sb, g_fx):
    t = qkv.shape[2]
    o_sb = _attn_call(_sb_prompt_kernel, "sb_prompt", SB_TILES, t, HEAD_DIM, lambda tq, tk: [])(qkv, qkv, qkv, g_sb)
    fox_tk = min(FOX_TILES[1], t)
    c_specs = lambda tq, tk: [pl.BlockSpec((1, tq, 1), lambda h, i: (h, i, 0)),
                              pl.BlockSpec((1, t // tk, 1, tk), lambda h, i: (h, 0, 0, 0))]
    o_fx = _attn_call(_fox_prompt_kernel, "fox_prompt", FOX_TILES, t, 2 * HEAD_DIM, c_specs)(
        qkv, qkv, qkv, c_rows.reshape(N_HEADS, t, 1), c_rows.reshape(N_HEADS, t // fox_tk, 1, fox_tk), g_fx)
    return o_sb, o_fx


def _sb_sample_kernel(q_ref, k_ref, v_ref, pk_ref, pv_ref, g_ref, o_ref, *, blk):
    tq = q_ref.shape[3]
    past = pk_ref.shape[3]
    suffix_new, suffix = _suffix_ones(tq), _suffix_ones(blk)
    blocks = range(past // blk)
    for hh in range(q_ref.shape[1]):
        q = q_ref[0, hh, 0]
        states = [(jnp.zeros((tq, 1), F32), jnp.zeros((tq, HEAD_DIM), F32))]
        (carry, acc), = _sb_step([q], k_ref[0, hh, 0], v_ref[0, hh, 0], suffix_new, states,
                                 _diag_masks(0, tq, tq, 1, True))
        k_blk = lambda j: pk_ref[0, 0, hh, j * blk:(j + 1) * blk, :].astype(BF16)
        v_blk = lambda j: pv_ref[0, 0, hh, j * blk:(j + 1) * blk, :].astype(BF16)
        z = [_dot_nt(q, k_blk(j)) for j in blocks]
        incl = [_dot(_softplus2(z[j]).astype(BF16), suffix) for j in blocks]
        for j in reversed(blocks):
            acc = acc + _dot(jnp.exp2(z[j] - incl[j] - carry).astype(BF16), v_blk(j))
            carry = carry + incl[j][:, :1]
        o_ref[:, hh * HEAD_DIM:(hh + 1) * HEAD_DIM] = _head_norm(acc, g_ref[hh])


def _fox_sample_kernel(q_ref, k_ref, v_ref, pk_ref, pv_ref, cq_ref, cn_ref, cp_ref, g_ref, o_ref, *, blk):
    tq = q_ref.shape[3]
    past = pk_ref.shape[3]
    blocks = range(past // blk)
    (mask,) = _diag_masks(0, tq, tq, 1, False)
    for hh in range(q_ref.shape[1]):
        q = q_ref[0, hh, 0]
        cq = cq_ref[0, hh]
        k_blk = lambda j: pk_ref[0, 0, hh, j * blk:(j + 1) * blk, :].astype(BF16)
        v_blk = lambda j: pv_ref[0, 0, hh, j * blk:(j + 1) * blk, :].astype(BF16)
        s_new = jnp.where(mask, _dot_nt(q, k_ref[0, hh, 0]) + (cq - cn_ref[0, hh]), -jnp.inf)
        s = [_dot_nt(q, k_blk(j)) + (cq - cp_ref[0, hh, j]) for j in blocks]
        m = jnp.max(s_new, axis=-1, keepdims=True)
        for j in blocks:
            m = jnp.maximum(m, jnp.max(s[j], axis=-1, keepdims=True))
        p_new = jnp.exp2(s_new - m)
        l = jnp.sum(p_new, axis=-1, keepdims=True)
        acc = _dot(p_new.astype(BF16), v_ref[0, hh, 0])
        for j in blocks:
            p = jnp.exp2(s[j] - m)
            l = l + jnp.sum(p, axis=-1, keepdims=True)
            acc = acc + _dot(p.astype(BF16), v_blk(j))
        o_ref[:, hh * HEAD_DIM:(hh + 1) * HEAD_DIM] = _head_norm(acc / l, g_ref[hh])


SAMPLE_HEADS_PER_STEP = 8


def _sample_attention(qkv, past_sb_k, past_sb_v, past_fx_k, past_fx_v, c_all, g_sb, g_fx, blk):
    b, _, past, _ = past_sb_k.shape
    tq = qkv.shape[2] // b
    hp = SAMPLE_HEADS_PER_STEP
    grid = (b, N_HEADS // hp)
    qkv = qkv.reshape(6, N_HEADS, b, tq, HEAD_DIM)
    new_spec = lambda n: pl.BlockSpec((1, hp, 1, tq, HEAD_DIM), lambda bi, h: (n, h, bi, 0, 0))
    past_spec = pl.BlockSpec((1, 1, hp, past, HEAD_DIM), lambda bi, h: (0, bi, h, 0, 0))
    g_spec = pl.BlockSpec((hp, 1, HEAD_DIM), lambda bi, h: (h, 0, 0))
    o_spec = pl.BlockSpec((tq, hp * HEAD_DIM), lambda bi, h: (bi, h))
    o_shape = jax.ShapeDtypeStruct((b * tq, D_GROUP), BF16)
    o_sb = pl.pallas_call(
        functools.partial(_sb_sample_kernel, blk=blk),
        grid=grid,
        in_specs=[new_spec(0), new_spec(1), new_spec(2), past_spec, past_spec, g_spec],
        out_specs=o_spec, out_shape=o_shape,
        compiler_params=_params(("parallel", "parallel")),
        name="sb_sample",
    )(qkv, qkv, qkv, past_sb_k[None], past_sb_v[None], g_sb)
    n_blk = past // blk
    cq = c_all[:, :, past:].reshape(b, N_HEADS, tq, 1)
    cn = c_all[:, :, past:].reshape(b, N_HEADS, 1, tq)
    cp = c_all[:, :, :past].reshape(b, N_HEADS, n_blk, 1, blk)
    o_fx = pl.pallas_call(
        functools.partial(_fox_sample_kernel, blk=blk),
        grid=grid,
        in_specs=[new_spec(3), new_spec(4), new_spec(5), past_spec, past_spec,
                  pl.BlockSpec((1, hp, tq, 1), lambda bi, h: (bi, h, 0, 0)),
                  pl.BlockSpec((1, hp, 1, tq), lambda bi, h: (bi, h, 0, 0)),
                  pl.BlockSpec((1, hp, n_blk, 1, blk), lambda bi, h: (bi, h, 0, 0, 0)),
                  g_spec],
        out_specs=o_spec, out_shape=o_shape,
        compiler_params=_params(("parallel", "parallel")),
        name="fox_sample",
    )(qkv, qkv, qkv, past_fx_k[None], past_fx_v[None], cq, cn, cp, g_fx)
    return o_sb, o_fx


def _out_proj_kernel(x_ref, osb_ref, ofx_ref, wsb_ref, wfx_ref, o_ref):
    o_ref[...] = x_ref[...] + _dot(osb_ref[...], wsb_ref[...]) + _dot(ofx_ref[...], wfx_ref[...])


def _out_proj(x, o_sb, o_fx, w_out, tm):
    m, d = x.shape
    row = lambda i: (i, 0)
    return pl.pallas_call(
        _out_proj_kernel,
        grid=(m // tm,),
        in_specs=[pl.BlockSpec((tm, d), row), pl.BlockSpec((tm, D_GROUP), row), pl.BlockSpec((tm, D_GROUP), row),
                  pl.BlockSpec((D_GROUP, d), lambda i: (0, 0)), pl.BlockSpec((D_GROUP, d), lambda i: (1, 0))],
        out_specs=pl.BlockSpec((tm, d), row),
        out_shape=jax.ShapeDtypeStruct((m, d), F32),
        compiler_params=_params(("parallel",)),
        name="out_proj",
    )(x, o_sb, o_fx, w_out, w_out)


def _ple_kernel(x_ref, xn_ref, p_ref, wg_ref, wp_ref, gf_ref, o_ref):
    gate = jax.nn.sigmoid(_dot(xn_ref[...], wg_ref[...]))
    emb = _dot(p_ref[...].astype(BF16), wp_ref[...])
    o_ref[...] = _rms(x_ref[...] + gate * emb, gf_ref[...])


def _ple_final(x, xn, p, w_gate, w_proj, g_final, tm):
    m, d = x.shape
    ple = p.shape[1]
    row = lambda i: (i, 0)
    full = lambda i: (0, 0)
    return pl.pallas_call(
        _ple_kernel,
        grid=(m // tm,),
        in_specs=[pl.BlockSpec((tm, d), row), pl.BlockSpec((tm, d), row), pl.BlockSpec((tm, ple), row),
                  pl.BlockSpec((d, d), full), pl.BlockSpec((ple, d), full), pl.BlockSpec((1, d), full)],
        out_specs=pl.BlockSpec((tm, d), row),
        out_shape=jax.ShapeDtypeStruct((m, d), F32),
        compiler_params=_params(("parallel",)),
        name="ple_final",
    )(x, xn, p, w_gate, w_proj, g_final)


def _pad_lanes(x, n):
    return jnp.pad(x, ((0, 0), (0, n - x.shape[1])))


ROW_TILE = 512
FF_TILE = 512
PLE_ROW_TILE = 256
SAMPLE_KEY_BLOCK = 256


def _run_group(x, p, past, w):
    b, t, d = x.shape
    m = b * t
    tm = min(ROW_TILE, m)
    row = lambda g: g.reshape(1, -1)
    x0 = x.reshape(m, d)
    x1, xn = _ffn(x0, row(w["g_ffn1"]), w["ffn1_gate"], w["ffn1_up"], w["ffn1_down"], row(w["g_mix"]), tm, FF_TILE)
    qkv, k_sb, v_sb, k_fx, v_fx, lf_cols = _proj(xn, w["in_qkv"], w["in_f"], w["b_f"], tm)
    lf = lf_cols[:, :N_HEADS].reshape(b, t, N_HEADS).transpose(0, 2, 1)
    g_sb = w["g_out_sb"].reshape(N_HEADS, 1, HEAD_DIM)
    g_fx = w["g_out_fox"].reshape(N_HEADS, 1, HEAD_DIM)
    if past is None:
        c_rows = _cumsum_rows(lf.reshape(N_HEADS, t))
        o_sb, o_fx = _prompt_attention(qkv, c_rows, g_sb, g_fx)
    else:
        lf_all = jnp.concatenate([past[4], lf], axis=-1).reshape(b * N_HEADS, -1)
        total = lf_all.shape[1]
        padded = -(-total // LANES) * LANES
        c_all = _cumsum_rows(_pad_lanes(lf_all, padded))[:, :total].reshape(b, N_HEADS, total)
        o_sb, o_fx = _sample_attention(qkv, past[0], past[1], past[2], past[3], c_all, g_sb, g_fx, SAMPLE_KEY_BLOCK)
    x2 = _out_proj(x1, o_sb, o_fx, w["out"], tm)
    x3, xn3 = _ffn(x2, row(w["g_ffn2"]), w["ffn2_gate"], w["ffn2_up"], w["ffn2_down"], row(w["g_ple"]), tm, FF_TILE)
    y = _ple_final(x3, xn3, p.reshape(m, -1), w["ple_gate"], w["ple_proj"], row(w["g_final"]), min(PLE_ROW_TILE, m))
    heads = lambda a: a.reshape(N_HEADS, b, t, HEAD_DIM).transpose(1, 0, 2, 3)[None]
    return y.reshape(b, t, d), heads(k_sb), heads(v_sb), heads(k_fx), heads(v_fx), lf[None]


def kernel(x_prompt, x_sample, cache_sb_k, cache_sb_v, cache_fox_k, cache_fox_v, cache_fox_logf, p_prompt, p_sample, g_ffn1, w_ffn1_gate, w_ffn1_up, w_ffn1_down, g_mix, w_in, b_forget, g_out_sb, g_out_fox, w_out, g_ffn2, w_ffn2_gate, w_ffn2_up, w_ffn2_down, g_ple, w_ple_gate, w_ple_proj, g_final):
    assert g_ffn1.shape[0] == 1, "single-layer trunk"
    bf = lambda a: a.astype(BF16)
    n_qkv = 6 * D_GROUP
    w = {
        "g_ffn1": g_ffn1[0], "ffn1_gate": bf(w_ffn1_gate[0]), "ffn1_up": bf(w_ffn1_up[0]), "ffn1_down": bf(w_ffn1_down[0]),
        "g_mix": g_mix[0], "in_qkv": bf(w_in[0]),
        "in_f": _pad_lanes(bf(w_in[0][:, n_qkv:]), LANES), "b_f": _pad_lanes(b_forget[0][None, :], LANES),
        "g_out_sb": g_out_sb[0], "g_out_fox": g_out_fox[0],
        "out": bf(w_out[0]),
        "g_ffn2": g_ffn2[0], "ffn2_gate": bf(w_ffn2_gate[0]), "ffn2_up": bf(w_ffn2_up[0]), "ffn2_down": bf(w_ffn2_down[0]),
        "g_ple": g_ple[0], "ple_gate": bf(w_ple_gate[0]), "ple_proj": bf(w_ple_proj[0]), "g_final": g_final,
    }
    past = (cache_sb_k[0], cache_sb_v[0], cache_fox_k[0], cache_fox_v[0], cache_fox_logf[0])
    y_p, sb_k_p, sb_v_p, fx_k_p, fx_v_p, lf_p = _run_group(x_prompt, p_prompt[0], None, w)
    y_s, sb_k_s, sb_v_s, fx_k_s, fx_v_s, lf_s = _run_group(x_sample, p_sample[0], past, w)
    return (y_p, y_s, sb_k_p, sb_v_p, fx_k_p, fx_v_p, lf_p, sb_k_s, sb_v_s, fx_k_s, fx_v_s, lf_s)
```

```python
import functools
import math

import jax
import jax.numpy as jnp
from jax import lax
from jax.experimental import pallas as pl
from jax.experimental.pallas import tpu as pltpu

HEAD_DIM = 128
N_HEADS = 8
D_GROUP = N_HEADS * HEAD_DIM
MACARON = 0.5
EPS = 1e-6
SCALE = 1.0 / math.sqrt(HEAD_DIM)
LOG2E = math.log2(math.e)
LANES = 128
V7X_VMEM_LIMIT = 56 * 1024 * 1024

F32 = jnp.float32
BF16 = jnp.bfloat16


def _params(semantics):
    return pltpu.CompilerParams(dimension_semantics=semantics, vmem_limit_bytes=V7X_VMEM_LIMIT)


def _rms(x, g):
    return x * lax.rsqrt(jnp.mean(x * x, axis=-1, keepdims=True) + EPS) * g


def _dot(a, b):
    return jnp.dot(a, b, preferred_element_type=F32)


def _dot_nt(a, b):
    return lax.dot_general(a, b, (((1,), (1,)), ((), ())), preferred_element_type=F32)


def _split3(x):
    hi = x.astype(BF16)
    r = x - hi.astype(F32)
    mid = r.astype(BF16)
    lo = (r - mid.astype(F32)).astype(BF16)
    return hi, mid, lo


def _softplus2(z2):
    return jnp.maximum(z2, 0.0) + jnp.log(1.0 + jnp.exp2(-jnp.abs(z2))) * LOG2E


def _ffn_kernel(x_ref, g_ref, wg_ref, wu_ref, wd_ref, gn_ref, o_ref, on_ref, xn_ref):
    f = pl.program_id(1)

    @pl.when(f == 0)
    def _():
        xn_ref[...] = _rms(x_ref[...], g_ref[...]).astype(BF16)
        o_ref[...] = jnp.zeros_like(o_ref)

    xn = xn_ref[...]
    gate = _dot(xn, wg_ref[...])
    up = _dot(xn, wu_ref[...])
    h = (gate * jax.nn.sigmoid(gate) * up).astype(BF16)
    o_ref[...] += _dot(h, wd_ref[...])

    @pl.when(f == pl.num_programs(1) - 1)
    def _():
        y = x_ref[...] + MACARON * o_ref[...]
        o_ref[...] = y
        on_ref[...] = _rms(y, gn_ref[...]).astype(BF16)


def _ffn(x, g, wg, wu, wd, g_next, tm, tf):
    m, d = x.shape
    d_ff = wg.shape[1]
    row = lambda i, f: (i, 0)
    return pl.pallas_call(
        _ffn_kernel,
        grid=(m // tm, d_ff // tf),
        in_specs=[
            pl.BlockSpec((tm, d), row),
            pl.BlockSpec((1, d), lambda i, f: (0, 0)),
            pl.BlockSpec((d, tf), lambda i, f: (0, f)),
            pl.BlockSpec((d, tf), lambda i, f: (0, f)),
            pl.BlockSpec((tf, d), lambda i, f: (f, 0)),
            pl.BlockSpec((1, d), lambda i, f: (0, 0)),
        ],
        out_specs=[pl.BlockSpec((tm, d), row), pl.BlockSpec((tm, d), row)],
        out_shape=[jax.ShapeDtypeStruct((m, d), F32), jax.ShapeDtypeStruct((m, d), BF16)],
        scratch_shapes=[pltpu.VMEM((tm, d), BF16)],
        compiler_params=_params(("parallel", "arbitrary")),
        name="ffn",
    )(x, g, wg, wu, wd, g_next)


def _proj_kernel(xn_ref, w_ref, wf_ref, bf_ref, qkv_ref, ksb_ref, vsb_ref, kfx_ref, vfx_ref, lf_ref):
    n = pl.program_id(1)
    xn = xn_ref[...]
    res = _dot(xn, w_ref[...])
    is_q = jnp.logical_or(n == 0, n == 3)
    res_mx = res * jnp.where(is_q, SCALE * LOG2E, 1.0)
    for h in range(N_HEADS):
        qkv_ref[0, h] = res_mx[:, h * HEAD_DIM:(h + 1) * HEAD_DIM].astype(BF16)

    def store_heads(ref):
        for h in range(N_HEADS):
            ref[h] = res[:, h * HEAD_DIM:(h + 1) * HEAD_DIM]

    for idx, ref in ((1, ksb_ref), (2, vsb_ref), (4, kfx_ref), (5, vfx_ref)):
        pl.when(n == idx)(functools.partial(store_heads, ref))

    @pl.when(n == 0)
    def _():
        fz = _dot(xn, wf_ref[...]) + bf_ref[...]
        lf_ref[...] = jnp.minimum(fz, 0.0) - jnp.log1p(jnp.exp(-jnp.abs(fz)))


def _proj(xn, w_qkv, w_f, b_f, tm):
    m, d = xn.shape
    head_f32 = jax.ShapeDtypeStruct((N_HEADS, m, HEAD_DIM), F32)
    head_spec = pl.BlockSpec((N_HEADS, tm, HEAD_DIM), lambda i, n: (0, i, 0))
    return pl.pallas_call(
        _proj_kernel,
        grid=(m // tm, 6),
        in_specs=[
            pl.BlockSpec((tm, d), lambda i, n: (i, 0)),
            pl.BlockSpec((d, D_GROUP), lambda i, n: (0, n)),
            pl.BlockSpec((d, LANES), lambda i, n: (0, 0)),
            pl.BlockSpec((1, LANES), lambda i, n: (0, 0)),
        ],
        out_specs=[
            pl.BlockSpec((1, N_HEADS, tm, HEAD_DIM), lambda i, n: (n, 0, i, 0)),
            head_spec, head_spec, head_spec, head_spec,
            pl.BlockSpec((tm, LANES), lambda i, n: (i, 0)),
        ],
        out_shape=[
            jax.ShapeDtypeStruct((6, N_HEADS, m, HEAD_DIM), BF16),
            head_f32, head_f32, head_f32, head_f32,
            jax.ShapeDtypeStruct((m, LANES), F32),
        ],
        compiler_params=_params(("parallel", "arbitrary")),
        name="in_proj",
    )(xn, w_qkv, w_f, b_f)


def _cumsum_kernel(x_ref, o_ref):
    n_chunks, rows, _ = x_ref.shape
    upper = (lax.broadcasted_iota(jnp.int32, (LANES, LANES), 0)
             <= lax.broadcasted_iota(jnp.int32, (LANES, LANES), 1)).astype(BF16)

    def body(c, carry):
        hi, mid, lo = _split3(x_ref[c])
        cs = _dot(hi, upper) + _dot(mid, upper) + _dot(lo, upper) + carry
        o_ref[c] = cs * LOG2E
        return cs[:, LANES - 1:LANES]

    lax.fori_loop(0, n_chunks, body, jnp.zeros((rows, 1), F32))


def _cumsum_rows(x):
    rows, length = x.shape
    xc = x.reshape(rows, length // LANES, LANES).transpose(1, 0, 2)
    out = pl.pallas_call(
        _cumsum_kernel,
        out_shape=jax.ShapeDtypeStruct(xc.shape, F32),
        compiler_params=pltpu.CompilerParams(vmem_limit_bytes=V7X_VMEM_LIMIT),
        name="cumsum",
    )(xc)
    return out.transpose(1, 0, 2).reshape(rows, length)


def _suffix_ones(n):
    return (lax.broadcasted_iota(jnp.int32, (n, n), 0)
            >= lax.broadcasted_iota(jnp.int32, (n, n), 1)).astype(BF16)


def _sb_step(qs, k, v, suffix, states, masks):
    live = [c for c, m in enumerate(masks) if not isinstance(m, str)]
    z = {c: _dot_nt(qs[c], k) for c in live}
    sp = {}
    for c in live:
        s = _softplus2(z[c])
        sp[c] = s if masks[c] is None else jnp.where(masks[c], s, 0.0)
    incl = {c: _dot(sp[c].astype(BF16), suffix) for c in live}
    out = list(states)
    for c in live:
        carry, acc = states[c]
        a = jnp.exp2(z[c] - incl[c] - carry)
        if masks[c] is not None:
            a = jnp.where(masks[c], a, 0.0)
        out[c] = (carry + incl[c][:, :1], acc + _dot(a.astype(BF16), v))
    return out


def _diag_masks(jj, tk, rc, n_chains, strict):
    masks = []
    for c in range(n_chains):
        lo_row, hi_row = c * rc, (c + 1) * rc - 1
        lo_key, hi_key = jj * tk, (jj + 1) * tk - 1
        if (lo_key >= hi_row) if strict else (lo_key > hi_row):
            masks.append("skip")
        elif (hi_key < lo_row) if strict else (hi_key <= lo_row):
            masks.append(None)
        else:
            rows = lax.broadcasted_iota(jnp.int32, (rc, tk), 0) + lo_row
            cols = lax.broadcasted_iota(jnp.int32, (rc, tk), 1) + lo_key
            masks.append(cols < rows if strict else cols <= rows)
    return masks


def _head_norm(o, g):
    return (o * lax.rsqrt(jnp.mean(o * o, axis=-1, keepdims=True) + EPS) * g).astype(BF16)


def _sb_prompt_kernel(q_ref, k_ref, v_ref, g_ref, o_ref, z_ref, a_ref, acc_ref, *, tq, tk, n_chains):
    qi = pl.program_id(1)
    rc = tq // n_chains
    ratio = tq // tk
    qs = [q_ref[0, 0, c * rc:(c + 1) * rc, :] for c in range(n_chains)]
    suffix = _suffix_ones(tk)

    def kv(j):
        start = pl.multiple_of(j * tk, tk)
        return k_ref[0, 0, pl.ds(start, tk), :], v_ref[0, 0, pl.ds(start, tk), :]

    states = [(jnp.zeros((rc, 1), F32), jnp.zeros((rc, HEAD_DIM), F32)) for _ in range(n_chains)]
    for jj in reversed(range(ratio)):
        k, v = kv(qi * ratio + jj)
        states = _sb_step(qs, k, v, suffix, states, _diag_masks(jj, tk, rc, n_chains, True))

    n_main = qi * ratio
    chains = range(n_chains)

    def put_scores(slot, j):
        k, _ = kv(jnp.maximum(j, 0))
        for c in chains:
            z_ref[slot, c] = _dot_nt(qs[c], k)

    def half_trip(src, dst, j, j_prev, carries):
        put_scores(dst, j - 1)
        _, v_prev = kv(j_prev)
        for c in chains:
            acc_ref[c] += _dot(a_ref[src, c], v_prev)
        z = [z_ref[src, c] for c in chains]
        incl = [_dot(_softplus2(z[c]).astype(BF16), suffix) for c in chains]
        for c in chains:
            a_ref[dst, c] = jnp.exp2(z[c] - incl[c] - carries[c]).astype(BF16)
        return [carries[c] + incl[c][:, :1] for c in chains]

    def body(pair, loop_state):
        j_prev, carries = loop_state
        j = n_main - 1 - 2 * pair
        carries = half_trip(0, 1, j, j_prev, carries)
        carries = half_trip(1, 0, j - 1, j, carries)
        return j - 1, carries

    put_scores(0, n_main - 1)
    for c in chains:
        a_ref[0, c] = jnp.zeros((rc, tk), BF16)
        acc_ref[c] = states[c][1]
    j_last, _ = lax.fori_loop(0, n_main // 2, body, (0, [states[c][0] for c in chains]))
    _, v_last = kv(j_last)
    for c in chains:
        o_ref[c * rc:(c + 1) * rc, :] = _head_norm(acc_ref[c] + _dot(a_ref[0, c], v_last), g_ref[0])


def _fox_prompt_kernel(q_ref, k_ref, v_ref, cq_ref, ck_ref, g_ref, o_ref, z_ref, p_ref, acc_ref, *, tq, tk, n_chains):
    qi = pl.program_id(1)
    rc = tq // n_chains
    ratio = tq // tk
    qs = [q_ref[0, 0, c * rc:(c + 1) * rc, :] for c in range(n_chains)]
    cqs = [cq_ref[0, c * rc:(c + 1) * rc, :] for c in range(n_chains)]

    def kv(j):
        start = pl.multiple_of(j * tk, tk)
        return k_ref[0, 0, pl.ds(start, tk), :], v_ref[0, 0, pl.ds(start, tk), :]

    lane = lax.broadcasted_iota(jnp.int32, (1, HEAD_DIM), 1)
    ones_col = jnp.broadcast_to(jnp.where(lane == 0, 1.0, 0.0).astype(BF16), (tk, HEAD_DIM))

    def values(j):
        _, v = kv(j)
        return jnp.concatenate([v, ones_col], axis=1)

    masks = [_diag_masks(jj, tk, rc, n_chains, False) for jj in range(ratio)]
    states = []
    for c in range(n_chains):
        seen = [jj for jj in range(ratio) if not isinstance(masks[jj][c], str)]
        s = []
        for jj in seen:
            sc = _dot_nt(qs[c], kv(qi * ratio + jj)[0]) + (cqs[c] - ck_ref[0, qi * ratio + jj])
            s.append(sc if masks[jj][c] is None else jnp.where(masks[jj][c], sc, -jnp.inf))
        m = functools.reduce(jnp.maximum, [jnp.max(x, axis=-1, keepdims=True) for x in s])
        acc = functools.reduce(jnp.add, [_dot(jnp.exp2(x - m).astype(BF16), values(qi * ratio + jj))
                                         for x, jj in zip(s, seen)])
        states.append((m, acc))

    n_main = qi * ratio
    chains = range(n_chains)

    def put_scores(slot, j):
        k, _ = kv(jnp.maximum(j, 0))
        for c in chains:
            z_ref[slot, c] = _dot_nt(qs[c], k)

    def half_trip(src, dst, j, j_prev, maxes):
        put_scores(dst, j - 1)
        v_prev = values(j_prev)
        ck = ck_ref[0, j]
        out = []
        for c in chains:
            s = z_ref[src, c] + (cqs[c] - ck)
            m_new = jnp.maximum(maxes[c], jnp.max(s, axis=-1, keepdims=True))
            acc_ref[c] = (acc_ref[c] + _dot(p_ref[src, c], v_prev)) * jnp.exp2(maxes[c] - m_new)
            p_ref[dst, c] = jnp.exp2(s - m_new).astype(BF16)
            out.append(m_new)
        return out

    def body(pair, loop_state):
        j_prev, maxes = loop_state
        j = n_main - 1 - 2 * pair
        maxes = half_trip(0, 1, j, j_prev, maxes)
        maxes = half_trip(1, 0, j - 1, j, maxes)
        return j - 1, maxes

    put_scores(0, n_main - 1)
    for c in chains:
        p_ref[0, c] = jnp.zeros((rc, tk), BF16)
        acc_ref[c] = states[c][1]
    j_last, _ = lax.fori_loop(0, n_main // 2, body, (0, [states[c][0] for c in chains]))
    v_last = values(j_last)
    for c in chains:
        acc = acc_ref[c] + _dot(p_ref[0, c], v_last)
        o_ref[c * rc:(c + 1) * rc, :] = _head_norm(acc[:, :HEAD_DIM] / acc[:, HEAD_DIM:HEAD_DIM + 1], g_ref[0])


SB_TILES = (2048, 256, 8)
FOX_TILES = (2048, 512, 4)


def _attn_call(kernel_fn, name, tiles, t, acc_lanes, extra_specs):
    tq, tk, n_chains = (min(tiles[0], t), min(tiles[1], t), tiles[2])
    rc = tq // n_chains
    assert (tq // tk) % 2 == 0, "the pipelined sweep walks key blocks in pairs"
    first = 0 if name == "sb_prompt" else 3
    q_spec = pl.BlockSpec((1, 1, tq, HEAD_DIM), lambda h, i: (first, h, i, 0))
    kv_spec = lambda n: pl.BlockSpec((1, 1, t, HEAD_DIM), lambda h, i: (n, h, 0, 0))
    g_spec = pl.BlockSpec((1, 1, HEAD_DIM), lambda h, i: (h, 0, 0))
    return pl.pallas_call(
        functools.partial(kernel_fn, tq=tq, tk=tk, n_chains=n_chains),
        grid=(N_HEADS, t // tq),
        in_specs=[q_spec, kv_spec(first + 1), kv_spec(first + 2)] + extra_specs(tq, tk) + [g_spec],
        out_specs=pl.BlockSpec((tq, HEAD_DIM), lambda h, i: (i, h)),
        out_shape=jax.ShapeDtypeStruct((t, D_GROUP), BF16),
        scratch_shapes=[pltpu.VMEM((2, n_chains, rc, tk), F32), pltpu.VMEM((2, n_chains, rc, tk), BF16),
                        pltpu.VMEM((n_chains, rc, acc_lanes), F32)],
        compiler_params=_params(("parallel", "arbitrary")),
        name=name,
    )


def _prompt_attention(qkv, c_rows, g_sb, g_fx):
    t = qkv.shape[2]
    o_sb = _attn_call(_sb_prompt_kernel, "sb_prompt", SB_TILES, t, HEAD_DIM, lambda tq, tk: [])(qkv, qkv, qkv, g_sb)
    fox_tk = min(FOX_TILES[1], t)
    c_specs = lambda tq, tk: [pl.BlockSpec((1, tq, 1), lambda h, i: (h, i, 0)),
                              pl.BlockSpec((1, t // tk, 1, tk), lambda h, i: (h, 0, 0, 0))]
    o_fx = _attn_call(_fox_prompt_kernel, "fox_prompt", FOX_TILES, t, 2 * HEAD_DIM, c_specs)(
        qkv, qkv, qkv, c_rows.reshape(N_HEADS, t, 1), c_rows.reshape(N_HEADS, t // fox_tk, 1, fox_tk), g_fx)
    return o_sb, o_fx


def _sb_sample_kernel(q_ref, k_ref, v_ref, pk_ref, pv_ref, g_ref, o_ref, *, blk):
    tq = q_ref.shape[3]
    past = pk_ref.shape[3]
    suffix_new, suffix = _suffix_ones(tq), _suffix_ones(blk)
    blocks = range(past // blk)
    for hh in range(q_ref.shape[1]):
        q = q_ref[0, hh, 0]
        states = [(jnp.zeros((tq, 1), F32), jnp.zeros((tq, HEAD_DIM), F32))]
        (carry, acc), = _sb_step([q], k_ref[0, hh, 0], v_ref[0, hh, 0], suffix_new, states,
                                 _diag_masks(0, tq, tq, 1, True))
        k_blk = lambda j: pk_ref[0, 0, hh, j * blk:(j + 1) * blk, :].astype(BF16)
        v_blk = lambda j: pv_ref[0, 0, hh, j * blk:(j + 1) * blk, :].astype(BF16)
        z = [_dot_nt(q, k_blk(j)) for j in blocks]
        incl = [_dot(_softplus2(z[j]).astype(BF16), suffix) for j in blocks]
        for j in reversed(blocks):
            acc = acc + _dot(jnp.exp2(z[j] - incl[j] - carry).astype(BF16), v_blk(j))
            carry = carry + incl[j][:, :1]
        o_ref[:, hh * HEAD_DIM:(hh + 1) * HEAD_DIM] = _head_norm(acc, g_ref[hh])


def _fox_sample_kernel(q_ref, k_ref, v_ref, pk_ref, pv_ref, cq_ref, cn_ref, cp_ref, g_ref, o_ref, *, blk):
    tq = q_ref.shape[3]
    past = pk_ref.shape[3]
    blocks = range(past // blk)
    (mask,) = _diag_masks(0, tq, tq, 1, False)
    for hh in range(q_ref.shape[1]):
        q = q_ref[0, hh, 0]
        cq = cq_ref[0, hh]
        k_blk = lambda j: pk_ref[0, 0, hh, j * blk:(j + 1) * blk, :].astype(BF16)
        v_blk = lambda j: pv_ref[0, 0, hh, j * blk:(j + 1) * blk, :].astype(BF16)
        s_new = jnp.where(mask, _dot_nt(q, k_ref[0, hh, 0]) + (cq - cn_ref[0, hh]), -jnp.inf)
        s = [_dot_nt(q, k_blk(j)) + (cq - cp_ref[0, hh, j]) for j in blocks]
        m = jnp.max(s_new, axis=-1, keepdims=True)
        for j in blocks:
            m = jnp.maximum(m, jnp.max(s[j], axis=-1, keepdims=True))
        p_new = jnp.exp2(s_new - m)
        l = jnp.sum(p_new, axis=-1, keepdims=True)
        acc = _dot(p_new.astype(BF16), v_ref[0, hh, 0])
        for j in blocks:
            p = jnp.exp2(s[j] - m)
            l = l + jnp.sum(p, axis=-1, keepdims=True)
            acc = acc + _dot(p.astype(BF16), v_blk(j))
        o_ref[:, hh * HEAD_DIM:(hh + 1) * HEAD_DIM] = _head_norm(acc / l, g_ref[hh])


SAMPLE_HEADS_PER_STEP = 8


def _sample_attention(qkv, past_sb_k, past_sb_v, past_fx_k, past_fx_v, c_all, g_sb, g_fx, blk):
    b, _, past, _ = past_sb_k.shape
    tq = qkv.shape[2] // b
    hp = SAMPLE_HEADS_PER_STEP
    grid = (b, N_HEADS // hp)
    qkv = qkv.reshape(6, N_HEADS, b, tq, HEAD_DIM)
    new_spec = lambda n: pl.BlockSpec((1, hp, 1, tq, HEAD_DIM), lambda bi, h: (n, h, bi, 0, 0))
    past_spec = pl.BlockSpec((1, 1, hp, past, HEAD_DIM), lambda bi, h: (0, bi, h, 0, 0))
    g_spec = pl.BlockSpec((hp, 1, HEAD_DIM), lambda bi, h: (h, 0, 0))
    o_spec = pl.BlockSpec((tq, hp * HEAD_DIM), lambda bi, h: (bi, h))
    o_shape = jax.ShapeDtypeStruct((b * tq, D_GROUP), BF16)
    o_sb = pl.pallas_call(
        functools.partial(_sb_sample_kernel, blk=blk),
        grid=grid,
        in_specs=[new_spec(0), new_spec(1), new_spec(2), past_spec, past_spec, g_spec],
        out_specs=o_spec, out_shape=o_shape,
        compiler_params=_params(("parallel", "parallel")),
        name="sb_sample",
    )(qkv, qkv, qkv, past_sb_k[None], past_sb_v[None], g_sb)
    n_blk = past // blk
    cq = c_all[:, :, past:].reshape(b, N_HEADS, tq, 1)
    cn = c_all[:, :, past:].reshape(b, N_HEADS, 1, tq)
    cp = c_all[:, :, :past].reshape(b, N_HEADS, n_blk, 1, blk)
    o_fx = pl.pallas_call(
        functools.partial(_fox_sample_kernel, blk=blk),
        grid=grid,
        in_specs=[new_spec(3), new_spec(4), new_spec(5), past_spec, past_spec,
                  pl.BlockSpec((1, hp, tq, 1), lambda bi, h: (bi, h, 0, 0)),
                  pl.BlockSpec((1, hp, 1, tq), lambda bi, h: (bi, h, 0, 0)),
                  pl.BlockSpec((1, hp, n_blk, 1, blk), lambda bi, h: (bi, h, 0, 0, 0)),
                  g_spec],
        out_specs=o_spec, out_shape=o_shape,
        compiler_params=_params(("parallel", "parallel")),
        name="fox_sample",
    )(qkv, qkv, qkv, past_fx_k[None], past_fx_v[None], cq, cn, cp, g_fx)
    return o_sb, o_fx


def _out_proj_kernel(x_ref, osb_ref, ofx_ref, wsb_ref, wfx_ref, o_ref):
    o_ref[...] = x_ref[...] + _dot(osb_ref[...], wsb_ref[...]) + _dot(ofx_ref[...], wfx_ref[...])


def _out_proj(x, o_sb, o_fx, w_out, tm):
    m, d = x.shape
    row = lambda i: (i, 0)
    return pl.pallas_call(
        _out_proj_kernel,
        grid=(m // tm,),
        in_specs=[pl.BlockSpec((tm, d), row), pl.BlockSpec((tm, D_GROUP), row), pl.BlockSpec((tm, D_GROUP), row),
                  pl.BlockSpec((D_GROUP, d), lambda i: (0, 0)), pl.BlockSpec((D_GROUP, d), lambda i: (1, 0))],
        out_specs=pl.BlockSpec((tm, d), row),
        out_shape=jax.ShapeDtypeStruct((m, d), F32),
        compiler_params=_params(("parallel",)),
        name="out_proj",
    )(x, o_sb, o_fx, w_out, w_out)


def _ple_kernel(x_ref, xn_ref, p_ref, wg_ref, wp_ref, gf_ref, o_ref):
    gate = jax.nn.sigmoid(_dot(xn_ref[...], wg_ref[...]))
    emb = _dot(p_ref[...].astype(BF16), wp_ref[...])
    o_ref[...] = _rms(x_ref[...] + gate * emb, gf_ref[...])


def _ple_final(x, xn, p, w_gate, w_proj, g_final, tm):
    m, d = x.shape
    ple = p.shape[1]
    row = lambda i: (i, 0)
    full = lambda i: (0, 0)
    return pl.pallas_call(
        _ple_kernel,
        grid=(m // tm,),
        in_specs=[pl.BlockSpec((tm, d), row), pl.BlockSpec((tm, d), row), pl.BlockSpec((tm, ple), row),
                  pl.BlockSpec((d, d), full), pl.BlockSpec((ple, d), full), pl.BlockSpec((1, d), full)],
        out_specs=pl.BlockSpec((tm, d), row),
        out_shape=jax.ShapeDtypeStruct((m, d), F32),
        compiler_params=_params(("parallel",)),
        name="ple_final",
    )(x, xn, p, w_gate, w_proj, g_final)


def _pad_lanes(x, n):
    return jnp.pad(x, ((0, 0), (0, n - x.shape[1])))


ROW_TILE = 512
FF_TILE = 512
PLE_ROW_TILE = 256
SAMPLE_KEY_BLOCK = 256


def _run_group(x, p, past, w):
    b, t, d = x.shape
    m = b * t
    tm = min(ROW_TILE, m)
    row = lambda g: g.reshape(1, -1)
    x0 = x.reshape(m, d)
    x1, xn = _ffn(x0, row(w["g_ffn1"]), w["ffn1_gate"], w["ffn1_up"], w["ffn1_down"], row(w["g_mix"]), tm, FF_TILE)
    qkv, k_sb, v_sb, k_fx, v_fx, lf_cols = _proj(xn, w["in_qkv"], w["in_f"], w["b_f"], tm)
    lf = lf_cols[:, :N_HEADS].reshape(b, t, N_HEADS).transpose(0, 2, 1)
    g_sb = w["g_out_sb"].reshape(N_HEADS, 1, HEAD_DIM)
    g_fx = w["g_out_fox"].reshape(N_HEADS, 1, HEAD_DIM)
    if past is None:
        c_rows = _cumsum_rows(lf.reshape(N_HEADS, t))
        o_sb, o_fx = _prompt_attention(qkv, c_rows, g_sb, g_fx)
    else:
        lf_all = jnp.concatenate([past[4], lf], axis=-1).reshape(b * N_HEADS, -1)
        total = lf_all.shape[1]
        padded = -(-total // LANES) * LANES
        c_all = _cumsum_rows(_pad_lanes(lf_all, padded))[:, :total].reshape(b, N_HEADS, total)
        o_sb, o_fx = _sample_attention(qkv, past[0], past[1], past[2], past[3], c_all, g_sb, g_fx, SAMPLE_KEY_BLOCK)
    x2 = _out_proj(x1, o_sb, o_fx, w["out"], tm)
    x3, xn3 = _ffn(x2, row(w["g_ffn2"]), w["ffn2_gate"], w["ffn2_up"], w["ffn2_down"], row(w["g_ple"]), tm, FF_TILE)
    y = _ple_final(x3, xn3, p.reshape(m, -1), w["ple_gate"], w["ple_proj"], row(w["g_final"]), min(PLE_ROW_TILE, m))
    heads = lambda a: a.reshape(N_HEADS, b, t, HEAD_DIM).transpose(1, 0, 2, 3)[None]
    return y.reshape(b, t, d), heads(k_sb), heads(v_sb), heads(k_fx), heads(v_fx), lf[None]


def kernel(x_prompt, x_sample, cache_sb_k, cache_sb_v, cache_fox_k, cache_fox_v, cache_fox_logf, p_prompt, p_sample, g_ffn1, w_ffn1_gate, w_ffn1_up, w_ffn1_down, g_mix, w_in, b_forget, g_out_sb, g_out_fox, w_out, g_ffn2, w_ffn2_gate, w_ffn2_up, w_ffn2_down, g_ple, w_ple_gate, w_ple_proj, g_final):
    assert g_ffn1.shape[0] == 1, "single-layer trunk"
    bf = lambda a: a.astype(BF16)
    n_qkv = 6 * D_GROUP
    w = {
        "g_ffn1": g_ffn1[0], "ffn1_gate": bf(w_ffn1_gate[0]), "ffn1_up": bf(w_ffn1_up[0]), "ffn1_down": bf(w_ffn1_down[0]),
        "g_mix": g_mix[0], "in_qkv": bf(w_in[0]),
        "in_f": _pad_lanes(bf(w_in[0][:, n_qkv:]), LANES), "b_f": _pad_lanes(b_forget[0][None, :], LANES),
        "g_out_sb": g_out_sb[0], "g_out_fox": g_out_fox[0],
        "out": bf(w_out[0]),
        "g_ffn2": g_ffn2[0], "ffn2_gate": bf(w_ffn2_gate[0]), "ffn2_up": bf(w_ffn2_up[0]), "ffn2_down": bf(w_ffn2_down[0]),
        "g_ple": g_ple[0], "ple_gate": bf(w_ple_gate[0]), "ple_proj": bf(w_ple_proj[0]), "g_final": g_final,
    }
    past = (cache_sb_k[0], cache_sb_v[0], cache_fox_k[0], cache_fox_v[0], cache_fox_logf[0])
    y_p, sb_k_p, sb_v_p, fx_k_p, fx_v_p, lf_p = _run_group(x_prompt, p_prompt[0], None, w)
    y_s, sb_k_s, sb_v_s, fx_k_s, fx_v_s, lf_s = _run_group(x_sample, p_sample[0], past, w)
    return (y_p, y_s, sb_k_p, sb_v_p, fx_k_p, fx_v_p, lf_p, sb_k_s, sb_v_s, fx_k_s, fx_v_s, lf_s)
```
